```python
import jax, jax.numpy as jnp
from jax import lax
import numpy as np

D_MODEL = 1024
BATCH = 16
SEQ = 4096
DEPTH = 1

CONV_WIDTH = D_MODEL
CONV_K = 3
SSM_EXPAND = 2
D_INNER = SSM_EXPAND * D_MODEL
SSM_HEAD_DIM = 64
SSM_HEADS = D_INNER // SSM_HEAD_DIM
SSM_GROUPS = 4
SSM_STATE = 128
SSM_CONV_K = 4
SSM_CHUNK = 256
D_XBC = D_INNER + 2 * SSM_GROUPS * SSM_STATE
N_EXPERTS = 32
TOP_K = 4
D_FF = D_MODEL
SWIGLU_LIMIT = 7.0
SWIGLU_ALPHA = 1.702
MOE_BLOCK = 256
EPS = 1e-6
N_MOD = 6

IN_SIZES = (CONV_WIDTH, CONV_WIDTH, CONV_WIDTH, D_INNER, D_XBC, SSM_HEADS, D_MODEL, D_MODEL)
IN_SPLITS = tuple(int(s) for s in np.cumsum(IN_SIZES)[:-1])
D_IN_PROJ = int(sum(IN_SIZES))

kernel_name = 'hybrid_shortconv_ssd_moe_adaln'


def rms_norm(x, g):
    xf = x.astype(jnp.float32)
    y = xf * lax.rsqrt(jnp.mean(xf * xf, axis=-1, keepdims=True) + EPS)
    return (y * g.astype(jnp.float32)).astype(x.dtype)


def causal_depthwise_conv(u, w):
    k, ch = w.shape
    return lax.conv_general_dilated(
        u, w[:, None, :].astype(u.dtype), window_strides=(1,), padding=[(k - 1, 0)],
        dimension_numbers=('NWC', 'WIO', 'NWC'), feature_group_count=ch)


def ssd_chunked_scan(xh, dt, a, bm, cm):
    bsz, seq, nh, hp = xh.shape
    ng, ns = bm.shape[2], bm.shape[3]
    hg = nh // ng
    pad = (-seq) % SSM_CHUNK
    padw = lambda t: jnp.pad(t, [(0, 0), (0, pad)] + [(0, 0)] * (t.ndim - 2))
    xh, dt, bm, cm = padw(xh), padw(dt), padw(bm), padw(cm)
    nc = (seq + pad) // SSM_CHUNK
    L = SSM_CHUNK
    x_c = jnp.moveaxis(xh.reshape(bsz, nc, L, ng, hg, hp), 1, 0)
    dt_c = jnp.moveaxis(dt.reshape(bsz, nc, L, ng, hg), 1, 0)
    b_c = jnp.moveaxis(bm.reshape(bsz, nc, L, ng, ns), 1, 0)
    c_c = jnp.moveaxis(cm.reshape(bsz, nc, L, ng, ns), 1, 0)
    a_g = a.reshape(ng, hg)
    causal = jnp.tril(jnp.ones((L, L), dtype=bool))[None, :, :, None, None]

    def step(state, inp):
        xc, dtc, bc, cc = inp
        a_cum = jnp.cumsum(dtc * a_g, axis=1)
        seg = a_cum[:, :, None] - a_cum[:, None, :]
        decay = jnp.exp(jnp.where(causal, seg, -jnp.inf))
        cb = jnp.einsum('bign,bjgn->bijg', cc, bc)
        scores = cb[..., None] * decay
        xdt = xc * dtc[..., None]
        y_diag = jnp.einsum('bijgh,bjghp->bighp', scores, xdt)
        y_off = jnp.einsum('bign,bghpn->bighp', cc, state) * jnp.exp(a_cum)[..., None]
        decay_end = jnp.exp(a_cum[:, -1:] - a_cum)
        new_state = (state * jnp.exp(a_cum[:, -1])[..., None, None]
                     + jnp.einsum('bjgh,bjghp,bjgn->bghpn', decay_end, xdt, bc))
        return new_state, y_diag + y_off

    init = jnp.zeros((bsz, ng, hg, hp, ns), jnp.float32)
    _, y = lax.scan(step, init, (x_c, dt_c, b_c, c_c))
    y = jnp.moveaxis(y, 0, 1).reshape(bsz, nc * L, nh, hp)
    return y[:, :seq]


def gated_group_rms_norm(y, z, g):
    yf = y.astype(jnp.float32) * jax.nn.silu(z.astype(jnp.float32))
    yg = yf.reshape(yf.shape[:-1] + (SSM_GROUPS, D_INNER // SSM_GROUPS))
    yg = yg * lax.rsqrt(jnp.mean(yg * yg, axis=-1, keepdims=True) + EPS)
    return yg.reshape(yf.shape) * g.astype(jnp.float32)


def token_mixer(h, w_in, w_sconv, w_sconv_out, w_ssm_conv, b_ssm_conv, dt_bias, a_log,
                d_skip, g_ssm_norm, w_ssm_out, w_o):
    bsz, seq, _ = h.shape
    proj = h @ w_in
    sb, sc, sx, z, xbc, dt_raw, gate_a, gate_b = jnp.split(proj, IN_SPLITS, axis=-1)
    y_a = (sc * causal_depthwise_conv(sb * sx, w_sconv)) @ w_sconv_out
    xbc = jax.nn.silu(causal_depthwise_conv(xbc, w_ssm_conv) + b_ssm_conv)
    xs, bs, cs = jnp.split(xbc, (D_INNER, D_INNER + SSM_GROUPS * SSM_STATE), axis=-1)
    dt = jax.nn.softplus(dt_raw.astype(jnp.float32) + dt_bias.astype(jnp.float32))
    a = -jnp.exp(a_log.astype(jnp.float32))
    xh = xs.reshape(bsz, seq, SSM_HEADS, SSM_HEAD_DIM).astype(jnp.float32)
    y = ssd_chunked_scan(xh, dt, a,
                         bs.reshape(bsz, seq, SSM_GROUPS, SSM_STATE).astype(jnp.float32),
                         cs.reshape(bsz, seq, SSM_GROUPS, SSM_STATE).astype(jnp.float32))
    y = (y + d_skip.astype(jnp.float32)[:, None] * xh).reshape(bsz, seq, D_INNER)
    y_b = gated_group_rms_norm(y, z, g_ssm_norm).astype(h.dtype) @ w_ssm_out
    m = jax.nn.sigmoid(gate_a) * y_a + jax.nn.sigmoid(gate_b) * y_b
    return m @ w_o


def moe_ffn(h, w_router, b_router, w_gu, b_gu, w_down, b_down):
    bsz, seq, d = h.shape
    n_tok = bsz * seq
    n_rows = n_tok * TOP_K
    hf = h.reshape(n_tok, d)
    logits = (hf @ w_router + b_router).astype(jnp.float32)
    top_val, top_idx = lax.top_k(logits, TOP_K)
    top_w = jax.nn.softmax(top_val, axis=-1)
    e_flat = top_idx.reshape(-1).astype(jnp.int32)
    tok_flat = jnp.arange(n_rows, dtype=jnp.int32) // TOP_K
    w_flat = top_w.reshape(-1)
    order = jnp.argsort(e_flat)
    e_sorted = e_flat[order]
    counts = jnp.bincount(e_flat, length=N_EXPERTS)
    padded = (counts + MOE_BLOCK - 1) // MOE_BLOCK * MOE_BLOCK
    start = jnp.cumsum(counts) - counts
    pend = jnp.cumsum(padded)
    pstart = pend - padded
    dest = pstart[e_sorted] + (jnp.arange(n_rows, dtype=jnp.int32) - start[e_sorted])
    n_blocks = -(-n_rows // MOE_BLOCK) + N_EXPERTS
    r_tot = n_blocks * MOE_BLOCK
    row_tok = jnp.zeros((r_tot,), jnp.int32).at[dest].set(tok_flat[order])
    row_w = jnp.zeros((r_tot,), jnp.float32).at[dest].set(w_flat[order])
    block_e = jnp.searchsorted(pend, jnp.arange(n_blocks) * MOE_BLOCK, side='right')
    block_e = jnp.minimum(block_e, N_EXPERTS - 1).astype(jnp.int32)

    def expert_block(args):
        toks, e = args
        gu = hf[toks] @ w_gu[e] + b_gu[e]
        gate = jnp.minimum(gu[:, :D_FF], SWIGLU_LIMIT)
        up = jnp.clip(gu[:, D_FF:], -SWIGLU_LIMIT, SWIGLU_LIMIT)
        act = (up + 1) * gate * jax.nn.sigmoid(SWIGLU_ALPHA * gate)
        return act @ w_down[e] + b_down[e]

    ys = lax.map(expert_block, (row_tok.reshape(n_blocks, MOE_BLOCK), block_e))
    ys = ys.reshape(r_tot, d) * row_w[:, None].astype(ys.dtype)
    out = jax.ops.segment_sum(ys, row_tok, num_segments=n_tok)
    return out.reshape(bsz, seq, d)


def setup_inputs(seed: int = 0) -> dict:
    key = jax.random.key(seed)
    k = jax.random.split(key, 24)
    n = lambda kk, shape, s: jax.random.normal(kk, shape, jnp.float32) * s
    dt0 = jnp.exp(jax.random.uniform(k[8], (DEPTH, SSM_HEADS), jnp.float32,
                                     np.log(1e-3), np.log(1e-1)))
    return {
        'x': n(k[0], (BATCH, SEQ, D_MODEL), 1.0),
        'c': n(k[1], (BATCH, D_MODEL), 1.0),
        'w_ada': n(k[2], (DEPTH, D_MODEL, N_MOD * D_MODEL), 0.5 * D_MODEL ** -0.5),
        'b_ada': n(k[3], (DEPTH, N_MOD * D_MODEL), 0.02),
        'g_mix': 1.0 + n(k[4], (DEPTH, D_MODEL), 0.05),
        'w_in': n(k[5], (DEPTH, D_MODEL, D_IN_PROJ), D_MODEL ** -0.5),
        'w_sconv': n(k[6], (DEPTH, CONV_K, CONV_WIDTH), CONV_K ** -0.5),
        'w_sconv_out': n(k[7], (DEPTH, CONV_WIDTH, D_MODEL), CONV_WIDTH ** -0.5),
        'w_ssm_conv': n(k[9], (DEPTH, SSM_CONV_K, D_XBC), SSM_CONV_K ** -0.5),
        'b_ssm_conv': n(k[10], (DEPTH, D_XBC), 0.02),
        'dt_bias': dt0 + jnp.log(-jnp.expm1(-dt0)),
        'a_log': jnp.log(jax.random.uniform(k[11], (DEPTH, SSM_HEADS), jnp.float32, 1.0, 16.0)),
        'd_skip': 1.0 + n(k[12], (DEPTH, SSM_HEADS), 0.05),
        'g_ssm_norm': 1.0 + n(k[13], (DEPTH, D_INNER), 0.05),
        'w_ssm_out': n(k[14], (DEPTH, D_INNER, D_MODEL), D_INNER ** -0.5),
        'w_o': n(k[15], (DEPTH, D_MODEL, D_MODEL), D_MODEL ** -0.5),
        'g_ffn': 1.0 + n(k[16], (DEPTH, D_MODEL), 0.05),
        'w_router': n(k[17], (DEPTH, D_MODEL, N_EXPERTS), D_MODEL ** -0.5),
        'b_router': n(k[18], (DEPTH, N_EXPERTS), 0.01),
        'w_gu': n(k[19], (DEPTH, N_EXPERTS, D_MODEL, 2 * D_FF), D_MODEL ** -0.5),
        'b_gu': n(k[20], (DEPTH, N_EXPERTS, 2 * D_FF), 0.02),
        'w_down': n(k[21], (DEPTH, N_EXPERTS, D_FF, D_MODEL), D_FF ** -0.5),
        'b_down': n(k[22], (DEPTH, N_EXPERTS, D_MODEL), 0.02),
        'g_final': 1.0 + n(k[23], (D_MODEL,), 0.05),
    }


def reference(x, c, w_ada, b_ada, g_mix, w_in, w_sconv, w_sconv_out, w_ssm_conv, b_ssm_conv,
              dt_bias, a_log, d_skip, g_ssm_norm, w_ssm_out, w_o, g_ffn, w_router, b_router,
              w_gu, b_gu, w_down, b_down, g_final):
    cond = jax.nn.silu(c)
    for l in range(DEPTH):
        mod = (cond @ w_ada[l] + b_ada[l])[:, None, :]
        sh1, sc1, gt1, sh2, sc2, gt2 = jnp.split(mod, N_MOD, axis=-1)
        h = rms_norm(x, g_mix[l]) * (1 + sc1) + sh1
        x = x + gt1 * token_mixer(h, w_in[l], w_sconv[l], w_sconv_out[l], w_ssm_conv[l],
                                  b_ssm_conv[l], dt_bias[l], a_log[l], d_skip[l],
                                  g_ssm_norm[l], w_ssm_out[l], w_o[l])
        h = rms_norm(x, g_ffn[l]) * (1 + sc2) + sh2
        x = x + gt2 * moe_ffn(h, w_router[l], b_router[l], w_gu[l], b_gu[l], w_down[l], b_down[l])
    return rms_norm(x, g_final)
```

```python
import functools

import jax
import jax.numpy as jnp
from jax import lax
from jax.experimental import pallas as pl
from jax.experimental.pallas import tpu as pltpu

F32 = jnp.float32
BF16 = jnp.bfloat16

D_MODEL = 1024
CONV_K = 3
D_INNER = 2048
SSM_HEAD_DIM = 64
SSM_HEADS = 32
SSM_GROUPS = 4
SSM_STATE = 128
SSM_CONV_K = 4
SSM_CHUNK = 256
D_XBC = D_INNER + 2 * SSM_GROUPS * SSM_STATE
N_EXPERTS = 32
TOP_K = 4
D_FF = D_MODEL
SWIGLU_LIMIT = 7.0
SWIGLU_ALPHA = 1.702
MOE_BLOCK = 256
EPS = 1e-6
N_MOD = 6

HEADS_PER_GROUP = SSM_HEADS // SSM_GROUPS
GROUP_WIDTH = D_INNER // SSM_GROUPS
LANES = 128
HALO = 8

P_Z = 0
P_XBC = P_Z + D_INNER
P_SB = P_XBC + D_XBC
P_SC = P_SB + D_MODEL
P_SX = P_SC + D_MODEL
P_GA = P_SX + D_MODEL
P_GB = P_GA + D_MODEL
P_TOTAL = P_GB + D_MODEL

VMEM_LIMIT = 56 * 1024 * 1024


def _params(sem):
    return pltpu.CompilerParams(dimension_semantics=sem, vmem_limit_bytes=VMEM_LIMIT)


def _rms(x, g):
    return x * lax.rsqrt(jnp.mean(x * x, axis=-1, keepdims=True) + EPS) * g


def _silu(x):
    return x * jax.nn.sigmoid(x)


def _mod_kernel(c_ref, w_ref, b_ref, o_ref):
    c = c_ref[...]
    cond = _silu(c)
    o_ref[...] = jnp.dot(cond, w_ref[...], preferred_element_type=F32,
                         precision=lax.Precision.HIGHEST) + b_ref[...]


def _modulation(c, w_ada, b_ada):
    bsz, d = c.shape
    n = w_ada.shape[1]
    bn = 1024
    return pl.pallas_call(
        _mod_kernel,
        grid=(n // bn,),
        in_specs=[pl.BlockSpec((bsz, d), lambda j: (0, 0)),
                  pl.BlockSpec((d, bn), lambda j: (0, j)),
                  pl.BlockSpec((1, bn), lambda j: (0, j))],
        out_specs=pl.BlockSpec((bsz, bn), lambda j: (0, j)),
        out_shape=jax.ShapeDtypeStruct((bsz, n), F32),
        compiler_params=_params(("arbitrary",)),
        name="adaln_mod",
    )(c, w_ada, b_ada.reshape(1, n))


def _inproj_kernel(x_ref, mod_ref, g_ref, w_ref, wdt_ref, o_ref, dt_ref, h_scr):
    @pl.when(pl.program_id(1) == 0)
    def _():
        sh = mod_ref[0, 0:1, :]
        sc = mod_ref[0, 1:2, :]
        h = (_rms(x_ref[...], g_ref[...]) * (1.0 + sc) + sh).astype(BF16)
        h_scr[...] = h
        dt_ref[...] = jnp.dot(h, wdt_ref[...], preferred_element_type=F32)

    o_ref[...] = jnp.dot(h_scr[...], w_ref[...], preferred_element_type=F32).astype(BF16)


def _in_projection(x2d, mod, g_mix, w_main, w_dt, seq):
    t, d = x2d.shape
    n = w_main.shape[1]
    tm = min(1024, seq)
    tn = 2048
    per_b = seq // tm
    return pl.pallas_call(
        _inproj_kernel,
        grid=(t // tm, n // tn),
        in_specs=[pl.BlockSpec((tm, d), lambda i, j: (i, 0)),
                  pl.BlockSpec((1, N_MOD, d), lambda i, j: (i // per_b, 0, 0)),
                  pl.BlockSpec((1, d), lambda i, j: (0, 0)),
                  pl.BlockSpec((d, tn), lambda i, j: (0, j)),
                  pl.BlockSpec((d, LANES), lambda i, j: (0, 0))],
        out_specs=[pl.BlockSpec((tm, tn), lambda i, j: (i, j)),
                   pl.BlockSpec((tm, LANES), lambda i, j: (i, 0))],
        out_shape=[jax.ShapeDtypeStruct((t, n), BF16),
                   jax.ShapeDtypeStruct((t, LANES), F32)],
        scratch_shapes=[pltpu.VMEM((tm, d), BF16)],
        compiler_params=_params(("arbitrary", "arbitrary")),
        name="in_proj",
    )(x2d, mod, g_mix.reshape(1, d), w_main, w_dt)


def _split3(v):
    hi = v.astype(BF16)
    r1 = v - hi.astype(F32)
    mid = r1.astype(BF16)
    lo = (r1 - mid.astype(F32)).astype(BF16)
    return hi, mid, lo


def _mixer_kernel(proj_ref, dt_ref, x_ref, mod_ref, wsc_ref, wcv_ref, bcv_ref, dtb_ref, alog_ref,
                  dskip_ref, gn_ref, e_ref, wa_ref, wb_ref, wo_ref, o_ref,
                  s_scr, hx_scr, hu_scr, extx, extu):
    L = SSM_CHUNK

    @pl.when(pl.program_id(1) == 0)
    def _():
        s_scr[...] = jnp.zeros_like(s_scr)
        hx_scr[...] = jnp.zeros_like(hx_scr)
        hu_scr[...] = jnp.zeros_like(hu_scr)

    def col(lo, width):
        return proj_ref[:, lo:lo + width].astype(F32)

    u = col(P_SB, D_MODEL) * col(P_SX, D_MODEL)
    extu[0:HALO, :] = hu_scr[...]
    extu[HALO:HALO + L, :] = u
    conv_a = wsc_ref[CONV_K - 1:CONV_K, :] * u
    for k in range(CONV_K - 1):
        back = CONV_K - 1 - k
        conv_a = conv_a + wsc_ref[k:k + 1, :] * extu[HALO - back:HALO - back + L, :]
    hu_scr[...] = extu[L:L + HALO, :]
    ya_in = (col(P_SC, D_MODEL) * conv_a).astype(BF16)
    y_a = jnp.dot(ya_in, wa_ref[...], preferred_element_type=F32)

    xbc = col(P_XBC, D_XBC)
    extx[0:HALO, :] = hx_scr[...]
    extx[HALO:HALO + L, :] = xbc
    cv = wcv_ref[SSM_CONV_K - 1:SSM_CONV_K, :] * xbc + bcv_ref[...]
    for k in range(SSM_CONV_K - 1):
        back = SSM_CONV_K - 1 - k
        cv = cv + wcv_ref[k:k + 1, :] * extx[HALO - back:HALO - back + L, :]
    hx_scr[...] = extx[L:L + HALO, :]
    act = _silu(cv)
    xs = act[:, :D_INNER]
    bs = act[:, D_INNER:D_INNER + SSM_GROUPS * SSM_STATE]
    cs = act[:, D_INNER + SSM_GROUPS * SSM_STATE:]

    dt = jax.nn.softplus(dt_ref[...] + dtb_ref[...])
    a = -jnp.exp(alog_ref[...])
    da = dt * a
    row = lax.broadcasted_iota(jnp.int32, (L, L), 0)
    cidx = lax.broadcasted_iota(jnp.int32, (L, L), 1)
    causal = cidx <= row
    tri = jnp.where(causal, 1.0, 0.0).astype(BF16)
    a_cum = sum(jnp.dot(tri, p, preferred_element_type=F32) for p in _split3(da))
    a_cum_t = a_cum.T
    ea = jnp.exp(a_cum)
    de = jnp.exp(a_cum[L - 1:L, :] - a_cum)
    e_mat = e_ref[...]
    dt_e = jnp.dot(dt.astype(BF16), e_mat, preferred_element_type=F32)
    w_e = jnp.dot((dt * de).astype(BF16), e_mat, preferred_element_type=F32)
    ea_e = jnp.dot(ea.astype(BF16), e_mat, preferred_element_type=F32)
    xdt = (xs * dt_e).astype(BF16)
    xw = (xs * w_e).astype(BF16)
    bs_t = bs.T
    lane = lax.broadcasted_iota(jnp.int32, (L, 2 * SSM_HEAD_DIM), 1)
    lo_half = lane < SSM_HEAD_DIM

    y_groups = []
    for g in range(SSM_GROUPS):
        cg = cs[:, g * SSM_STATE:(g + 1) * SSM_STATE].astype(BF16)
        bg = bs[:, g * SSM_STATE:(g + 1) * SSM_STATE].astype(BF16)
        bg_t = bs_t[g * SSM_STATE:(g + 1) * SSM_STATE, :].astype(BF16)
        cb = lax.dot_general(cg, bg, (((1,), (1,)), ((), ())), preferred_element_type=F32)
        s_old = s_scr[g]
        gsl = slice(g * GROUP_WIDTH, (g + 1) * GROUP_WIDTH)
        y_off = jnp.dot(cg, s_old.astype(BF16), preferred_element_type=F32) * ea_e[:, gsl]
        pieces = []
        for p in range(HEADS_PER_GROUP // 2):
            acc = None
            xpair = xdt[:, g * GROUP_WIDTH + p * 2 * SSM_HEAD_DIM:g * GROUP_WIDTH + (p + 1) * 2 * SSM_HEAD_DIM]
            for half in range(2):
                h = g * HEADS_PER_GROUP + 2 * p + half
                seg = a_cum[:, h:h + 1] - a_cum_t[h:h + 1, :]
                decay = jnp.exp(jnp.where(causal, seg, -jnp.inf))
                scores = (cb * decay).astype(BF16)
                keep = lo_half if half == 0 else jnp.logical_not(lo_half)
                part = jnp.dot(scores, jnp.where(keep, xpair, jnp.zeros_like(xpair)),
                               preferred_element_type=F32)
                acc = part if acc is None else acc + part
            pieces.append(acc)
        y_groups.append(jnp.concatenate(pieces, axis=1) + y_off)
        s_new = s_old * ea_e[L - 1:L, gsl] + jnp.dot(bg_t, xw[:, gsl], preferred_element_type=F32)
        s_scr[g] = s_new
    y = jnp.concatenate(y_groups, axis=1) + dskip_ref[...] * xs

    yf = y * _silu(col(P_Z, D_INNER))
    normed = []
    for g in range(SSM_GROUPS):
        blk = yf[:, g * GROUP_WIDTH:(g + 1) * GROUP_WIDTH]
        normed.append(blk * lax.rsqrt(jnp.mean(blk * blk, axis=-1, keepdims=True) + EPS))
    yn = (jnp.concatenate(normed, axis=1) * gn_ref[...]).astype(BF16)
    y_b = jnp.dot(yn, wb_ref[...], preferred_element_type=F32)
    m = jax.nn.sigmoid(col(P_GA, D_MODEL)) * y_a + jax.nn.sigmoid(col(P_GB, D_MODEL)) * y_b
    mix = jnp.dot(m.astype(BF16), wo_ref[...], preferred_element_type=F32)
    o_ref[...] = x_ref[...] + mod_ref[0, 2:3, :] * mix


def _token_mixer(proj, dt_raw, x2d, mod, w_sconv, w_ssm_conv, b_ssm_conv, dt_bias, a_log, d_skip,
                 g_ssm_norm, w_a, w_b, w_o, bsz, seq):
    t, d = x2d.shape
    L = SSM_CHUNK
    nc = seq // L
    pad_h = LANES - SSM_HEADS
    dtb = jnp.pad(dt_bias, (0, pad_h)).reshape(1, LANES)
    alog = jnp.pad(a_log, (0, pad_h)).reshape(1, LANES)
    dskip_e = jnp.repeat(d_skip, SSM_HEAD_DIM).reshape(1, D_INNER)
    expand = (jnp.arange(D_INNER)[None, :] // SSM_HEAD_DIM == jnp.arange(LANES)[:, None]).astype(BF16)
    full = lambda shape: pl.BlockSpec(shape, lambda b, c: (0,) * len(shape))
    return pl.pallas_call(
        _mixer_kernel,
        grid=(bsz, nc),
        in_specs=[pl.BlockSpec((L, P_TOTAL), lambda b, c: (b * nc + c, 0)),
                  pl.BlockSpec((L, LANES), lambda b, c: (b * nc + c, 0)),
                  pl.BlockSpec((L, d), lambda b, c: (b * nc + c, 0)),
                  pl.BlockSpec((1, N_MOD, d), lambda b, c: (b, 0, 0)),
                  full((CONV_K, D_MODEL)), full((SSM_CONV_K, D_XBC)), full((1, D_XBC)),
                  full((1, LANES)), full((1, LANES)), full((1, D_INNER)), full((1, D_INNER)),
                  full((LANES, D_INNER)),
                  full((D_MODEL, D_MODEL)), full((D_INNER, D_MODEL)), full((D_MODEL, D_MODEL))],
        out_specs=pl.BlockSpec((L, d), lambda b, c: (b * nc + c, 0)),
        out_shape=jax.ShapeDtypeStruct((t, d), F32),
        scratch_shapes=[pltpu.VMEM((SSM_GROUPS, SSM_STATE, GROUP_WIDTH), F32),
                        pltpu.VMEM((HALO, D_XBC), F32),
                        pltpu.VMEM((HALO, D_MODEL), F32),
                        pltpu.VMEM((L + HALO, D_XBC), F32),
                        pltpu.VMEM((L + HALO, D_MODEL), F32)],
        compiler_params=_params(("arbitrary", "arbitrary")),
        name="token_mixer",
    )(proj, dt_raw, x2d, mod, w_sconv, w_ssm_conv, b_ssm_conv.reshape(1, D_XBC), dtb, alog, dskip_e,
      g_ssm_norm.reshape(1, D_INNER), expand, w_a, w_b, w_o)


def _ffn_input(x, mod_ref, g_ref):
    return _rms(x, g_ref[...]) * (1.0 + mod_ref[0, 4:5, :]) + mod_ref[0, 3:4, :]


def _router_kernel(x_ref, mod_ref, g_ref, wr_ref, br_ref, idx_ref, w_ref, rank_ref, cnt_ref, run_scr):
    tm = x_ref.shape[0]

    @pl.when(pl.program_id(0) == 0)
    def _():
        run_scr[...] = jnp.zeros_like(run_scr)

    h = _ffn_input(x_ref[...], mod_ref, g_ref)
    logits = jnp.dot(h, wr_ref[...], preferred_element_type=F32,
                     precision=lax.Precision.HIGHEST) + br_ref[...]
    lane = lax.broadcasted_iota(jnp.int32, (tm, LANES), 1)
    work = jnp.where(lane < N_EXPERTS, logits, -jnp.inf)
    vals, idxs, hots = [], [], []
    for _ in range(TOP_K):
        v = jnp.max(work, axis=-1, keepdims=True)
        i = jnp.min(jnp.where(work == v, lane, LANES), axis=-1, keepdims=True)
        hot = lane == i
        vals.append(v)
        idxs.append(i)
        hots.append(hot)
        work = jnp.where(hot, -jnp.inf, work)
    exps = [jnp.exp(v - vals[0]) for v in vals]
    denom = exps[0] + exps[1] + exps[2] + exps[3]

    onehot = jnp.zeros((tm, LANES), F32)
    for hot in hots:
        onehot = onehot + jnp.where(hot, 1.0, 0.0)
    r = lax.broadcasted_iota(jnp.int32, (tm, tm), 0)
    c = lax.broadcasted_iota(jnp.int32, (tm, tm), 1)
    strict = jnp.where(c < r, 1.0, 0.0).astype(BF16)
    before = jnp.dot(strict, onehot.astype(BF16), preferred_element_type=F32) + run_scr[...]
    for k in range(TOP_K):
        rank = jnp.sum(jnp.where(hots[k], before, 0.0), axis=-1, keepdims=True)
        idx_ref[:, k:k + 1] = idxs[k]
        w_ref[:, k:k + 1] = exps[k] / denom
        rank_ref[:, k:k + 1] = rank.astype(jnp.int32)
    run_scr[...] = run_scr[...] + jnp.sum(onehot, axis=0, keepdims=True)
    cnt_ref[...] = run_scr[...].astype(jnp.int32)


def _router(x1, mod, g_ffn, w_router, b_router, seq):
    t, d = x1.shape
    tm = min(512, seq)
    per_b = seq // tm
    wr = jnp.pad(w_router, ((0, 0), (0, LANES - N_EXPERTS)))
    br = jnp.pad(b_router, (0, LANES - N_EXPERTS)).reshape(1, LANES)
    slot = lambda dtype: jax.ShapeDtypeStruct((t, TOP_K), dtype)
    slot_spec = pl.BlockSpec((tm, TOP_K), lambda i: (i, 0))
    return pl.pallas_call(
        _router_kernel,
        grid=(t // tm,),
        in_specs=[pl.BlockSpec((tm, d), lambda i: (i, 0)),
                  pl.BlockSpec((1, N_MOD, d), lambda i: (i // per_b, 0, 0)),
                  pl.BlockSpec((1, d), lambda i: (0, 0)),
                  pl.BlockSpec((d, LANES), lambda i: (0, 0)),
                  pl.BlockSpec((1, LANES), lambda i: (0, 0))],
        out_specs=[slot_spec, slot_spec, slot_spec, pl.BlockSpec((1, LANES), lambda i: (0, 0))],
        out_shape=[slot(jnp.int32), slot(F32), slot(jnp.int32),
                   jax.ShapeDtypeStruct((1, LANES), jnp.int32)],
        scratch_shapes=[pltpu.VMEM((1, LANES), F32)],
        compiler_params=_params(("arbitrary",)),
        name="router",
    )(x1, mod, g_ffn.reshape(1, d), wr, br)


def _dest_kernel(pstart_ref, idx_ref, rank_ref, o_ref):
    idx = idx_ref[...]
    base = jnp.zeros_like(idx)
    for e in range(N_EXPERTS):
        base = jnp.where(idx == e, pstart_ref[e], base)
    o_ref[...] = base + rank_ref[...]


def _dest_rows(pstart, idx, rank):
    t = idx.shape[0]
    tm = min(2048, t)
    spec = pl.BlockSpec((tm, TOP_K), lambda i, ps: (i, 0))
    return pl.pallas_call(
        _dest_kernel,
        grid_spec=pltpu.PrefetchScalarGridSpec(
            num_scalar_prefetch=1, grid=(t // tm,), in_specs=[spec, spec], out_specs=spec),
        out_shape=jax.ShapeDtypeStruct((t, TOP_K), jnp.int32),
        compiler_params=_params(("arbitrary",)),
        name="dest_rows",
    )(pstart, idx, rank)


def _row_copy(src, src_row, dst, dst_row, sem):
    return pltpu.make_async_copy(src.at[pl.ds(src_row, 1), :], dst.at[pl.ds(dst_row, 1), :], sem)


def _dispatch_kernel(cnt_ref, pstart_ref, dest_ref, x_ref, mod_ref, g_ref, xs_ref, h_scr, z_scr, sem):
    tm = x_ref.shape[0]
    n_rows = tm * TOP_K
    h_scr[...] = _ffn_input(x_ref[...], mod_ref, g_ref)

    def issue(i, carry):
        _row_copy(h_scr, i // TOP_K, xs_ref, dest_ref[0, 0, i], sem).start()
        return carry

    lax.fori_loop(0, n_rows, issue, 0, unroll=8)

    def drain(i, carry):
        _row_copy(h_scr, 0, xs_ref, 0, sem).wait()
        return carry

    lax.fori_loop(0, n_rows, drain, 0, unroll=8)

    @pl.when(pl.program_id(0) == pl.num_programs(0) - 1)
    def _():
        z_scr[...] = jnp.zeros_like(z_scr)
        for e in range(N_EXPERTS):
            cnt = cnt_ref[e]
            n_pad = (MOE_BLOCK - cnt % MOE_BLOCK) % MOE_BLOCK
            base = pstart_ref[e] + cnt

            def zissue(j, carry):
                _row_copy(z_scr, 0, xs_ref, base + j, sem).start()
                return carry

            def zdrain(j, carry):
                _row_copy(z_scr, 0, xs_ref, 0, sem).wait()
                return carry

            lax.fori_loop(0, n_pad, zissue, 0)
            lax.fori_loop(0, n_pad, zdrain, 0)


def _dispatch(cnt, pstart, dest, x1, mod, g_ffn, r_tot, seq):
    t, d = x1.shape
    tm = min(256, seq)
    per_b = seq // tm
    dest3 = dest.reshape(t // tm, 1, tm * TOP_K)
    return pl.pallas_call(
        _dispatch_kernel,
        grid_spec=pltpu.PrefetchScalarGridSpec(
            num_scalar_prefetch=2, grid=(t // tm,),
            in_specs=[pl.BlockSpec((1, 1, tm * TOP_K), lambda i, c, p: (i, 0, 0), memory_space=pltpu.SMEM),
                      pl.BlockSpec((tm, d), lambda i, c, p: (i, 0)),
                      pl.BlockSpec((1, N_MOD, d), lambda i, c, p: (i // per_b, 0, 0)),
                      pl.BlockSpec((1, d), lambda i, c, p: (0, 0))],
            out_specs=pl.BlockSpec(memory_space=pl.ANY),
            scratch_shapes=[pltpu.VMEM((tm, d), F32), pltpu.VMEM((8, d), F32), pltpu.SemaphoreType.DMA]),
        out_shape=jax.ShapeDtypeStruct((r_tot, d), F32),
        compiler_params=_params(("arbitrary",)),
        name="dispatch",
    )(cnt, pstart, dest3, x1, mod, g_ffn.reshape(1, d))


def _expert_kernel(be_ref, nu_ref, x_ref, wgu_ref, bgu_ref, wd_ref, bd_ref, o_ref):
    @pl.when(pl.program_id(0) < nu_ref[0])
    def _():
        x = x_ref[...].astype(BF16)
        gu = jnp.dot(x, wgu_ref[0], preferred_element_type=F32) + bgu_ref[0]
        gate = jnp.minimum(gu[:, :D_FF], SWIGLU_LIMIT)
        up = jnp.clip(gu[:, D_FF:], -SWIGLU_LIMIT, SWIGLU_LIMIT)
        act = (up + 1.0) * gate * jax.nn.sigmoid(SWIGLU_ALPHA * gate)
        o_ref[...] = jnp.dot(act.astype(BF16), wd_ref[0], preferred_element_type=F32) + bd_ref[0]


def _expert_blocks(block_e, n_used, xs, w_gu, b_gu, w_down, b_down):
    r_tot, d = xs.shape
    n_blocks = r_tot // MOE_BLOCK
    row_map = lambda i, be, nu: (jnp.minimum(i, nu[0] - 1), 0)
    e_map = lambda i, be, nu: (be[i], 0, 0)
    return pl.pallas_call(
        _expert_kernel,
        grid_spec=pltpu.PrefetchScalarGridSpec(
            num_scalar_prefetch=2, grid=(n_blocks,),
            in_specs=[pl.BlockSpec((MOE_BLOCK, d), row_map),
                      pl.BlockSpec((1, d, 2 * D_FF), e_map),
                      pl.BlockSpec((1, 1, 2 * D_FF), e_map),
                      pl.BlockSpec((1, D_FF, d), e_map),
                      pl.BlockSpec((1, 1, d), e_map)],
            out_specs=pl.BlockSpec((MOE_BLOCK, d), row_map)),
        out_shape=jax.ShapeDtypeStruct((r_tot, d), F32),
        compiler_params=_params(("arbitrary",)),
        name="expert_blocks",
    )(block_e, n_used, xs, w_gu, b_gu.reshape(N_EXPERTS, 1, 2 * D_FF), w_down,
      b_down.reshape(N_EXPERTS, 1, d))


def _combine_kernel(dest_ref, y_ref, w_ref, x_ref, mod_ref, g_ref, o_ref, buf, sem, *, final_norm):
    tm = x_ref.shape[0]
    n_rows = tm * TOP_K

    def issue(i, carry):
        pltpu.make_async_copy(y_ref.at[pl.ds(dest_ref[0, 0, i], 1), :],
                              buf.at[i % TOP_K, pl.ds(i // TOP_K, 1), :], sem).start()
        return carry

    lax.fori_loop(0, n_rows, issue, 0, unroll=8)

    def drain(i, carry):
        pltpu.make_async_copy(y_ref.at[pl.ds(0, 1), :], buf.at[0, pl.ds(0, 1), :], sem).wait()
        return carry

    lax.fori_loop(0, n_rows, drain, 0, unroll=8)

    w = w_ref[...]
    moe = w[:, 0:1] * buf[0]
    for k in range(1, TOP_K):
        moe = moe + w[:, k:k + 1] * buf[k]
    x2 = x_ref[...] + mod_ref[0, 5:6, :] * moe
    o_ref[...] = _rms(x2, g_ref[...]) if final_norm else x2


def _combine(dest, y, top_w, x1, mod, g_final, seq, final_norm):
    t, d = x1.shape
    tm = min(256, seq)
    per_b = seq // tm
    dest3 = dest.reshape(t // tm, 1, tm * TOP_K)
    return pl.pallas_call(
        functools.partial(_combine_kernel, final_norm=final_norm),
        grid=(t // tm,),
        in_specs=[pl.BlockSpec((1, 1, tm * TOP_K), lambda i: (i, 0, 0), memory_space=pltpu.SMEM),
                  pl.BlockSpec(memory_space=pl.ANY),
                  pl.BlockSpec((tm, TOP_K), lambda i: (i, 0)),
                  pl.BlockSpec((tm, d), lambda i: (i, 0)),
                  pl.BlockSpec((1, N_MOD, d), lambda i: (i // per_b, 0, 0)),
                  pl.BlockSpec((1, d), lambda i: (0, 0))],
        out_specs=pl.BlockSpec((tm, d), lambda i: (i, 0)),
        out_shape=jax.ShapeDtypeStruct((t, d), F32),
        scratch_shapes=[pltpu.VMEM((TOP_K, tm, d), F32), pltpu.SemaphoreType.DMA],
        compiler_params=_params(("arbitrary",)),
        name="combine",
    )(dest3, y, top_w, x1, mod, g_final.reshape(1, d))


def _regroup_w_in(w_in):
    o_sb, o_sc, o_sx = 0, D_MODEL, 2 * D_MODEL
    o_z = 3 * D_MODEL
    o_xbc = o_z + D_INNER
    o_dt = o_xbc + D_XBC
    o_ga = o_dt + SSM_HEADS
    o_gb = o_ga + D_MODEL
    main = jnp.concatenate([w_in[:, o_z:o_xbc], w_in[:, o_xbc:o_dt], w_in[:, o_sb:o_sc], w_in[:, o_sc:o_sx],
                            w_in[:, o_sx:o_z], w_in[:, o_ga:o_gb], w_in[:, o_gb:o_gb + D_MODEL]], axis=1)
    w_dt = jnp.pad(w_in[:, o_dt:o_ga], ((0, 0), (0, LANES - SSM_HEADS)))
    return main.astype(BF16), w_dt.astype(BF16)


def kernel(x, c, w_ada, b_ada, g_mix, w_in, w_sconv, w_sconv_out, w_ssm_conv, b_ssm_conv, dt_bias, a_log,
           d_skip, g_ssm_norm, w_ssm_out, w_o, g_ffn, w_router, b_router, w_gu, b_gu, w_down, b_down,
           g_final):
    bsz, seq, d = x.shape
    depth = w_ada.shape[0]
    t = bsz * seq
    n_rows = t * TOP_K
    n_blocks = -(-n_rows // MOE_BLOCK) + N_EXPERTS
    r_tot = n_blocks * MOE_BLOCK
    xf = x.reshape(t, d)
    for l in range(depth):
        mod = _modulation(c, w_ada[l], b_ada[l]).reshape(bsz, N_MOD, d)
        w_main, w_dt = _regroup_w_in(w_in[l])
        proj, dt_raw = _in_projection(xf, mod, g_mix[l], w_main, w_dt, seq)
        x1 = _token_mixer(proj, dt_raw, xf, mod, w_sconv[l], w_ssm_conv[l], b_ssm_conv[l], dt_bias[l],
                          a_log[l], d_skip[l], g_ssm_norm[l], w_sconv_out[l].astype(BF16),
                          w_ssm_out[l].astype(BF16), w_o[l].astype(BF16), bsz, seq)
        top_idx, top_w, rank, counts = _router(x1, mod, g_ffn[l], w_router[l], b_router[l], seq)
        cnt = counts[0, :N_EXPERTS]
        padded = (cnt + MOE_BLOCK - 1) // MOE_BLOCK * MOE_BLOCK
        pend = jnp.cumsum(padded)
        pstart = (pend - padded).astype(jnp.int32)
        block_e = jnp.searchsorted(pend, jnp.arange(n_blocks, dtype=jnp.int32) * MOE_BLOCK, side='right')
        block_e = jnp.minimum(block_e, N_EXPERTS - 1).astype(jnp.int32)
        n_used = (pend[-1:] // MOE_BLOCK).astype(jnp.int32)
        dest = _dest_rows(pstart, top_idx, rank)
        xs = _dispatch(cnt, pstart, dest, x1, mod, g_ffn[l], r_tot, seq)
        ys = _expert_blocks(block_e, n_used, xs, w_gu[l].astype(BF16), b_gu[l], w_down[l].astype(BF16),
                            b_down[l])
        xf = _combine(dest, ys, top_w, x1, mod, g_final, seq, final_norm=(l == depth - 1))
    return xf.reshape(bsz, seq, d)
```

```python
import functools

import jax
import jax.numpy as jnp
from jax import lax
from jax.experimental import pallas as pl
from jax.experimental.pallas import tpu as pltpu

F32 = jnp.float32
BF16 = jnp.bfloat16

D_MODEL = 1024
CONV_K = 3
D_INNER = 2048
SSM_HEAD_DIM = 64
SSM_HEADS = 32
SSM_GROUPS = 4
SSM_STATE = 128
SSM_CONV_K = 4
SSM_CHUNK = 256
D_XBC = D_INNER + 2 * SSM_GROUPS * SSM_STATE
N_EXPERTS = 32
TOP_K = 4
D_FF = D_MODEL
SWIGLU_LIMIT = 7.0
SWIGLU_ALPHA = 1.702
EPS = 1e-6
N_MOD = 6

HEADS_PER_GROUP = SSM_HEADS // SSM_GROUPS
GROUP_WIDTH = D_INNER // SSM_GROUPS
LANES = 128
HALO = 8

MOE_BLOCK = 256
MOE_TILE = 512
SEG_ALIGN = 16
SORT_ROWS = -(-(MOE_TILE * TOP_K + N_EXPERTS * (SEG_ALIGN - 1)) // MOE_BLOCK) * MOE_BLOCK
SEG_BITS = tuple(SEG_ALIGN << b for b in reversed(range((MOE_TILE // SEG_ALIGN).bit_length())))

P_Z = 0
P_XBC = P_Z + D_INNER
P_SB = P_XBC + D_XBC
P_SC = P_SB + D_MODEL
P_SX = P_SC + D_MODEL
P_GA = P_SX + D_MODEL
P_GB = P_GA + D_MODEL
P_TOTAL = P_GB + D_MODEL

VMEM_LIMIT = 56 * 1024 * 1024


def _params(sem):
    return pltpu.CompilerParams(dimension_semantics=sem, vmem_limit_bytes=VMEM_LIMIT)


def _rms(x, g):
    return x * lax.rsqrt(jnp.mean(x * x, axis=-1, keepdims=True) + EPS) * g


def _silu(x):
    return x * jax.nn.sigmoid(x)


def _mod_kernel(c_ref, w_ref, b_ref, o_ref):
    c = c_ref[...]
    cond = _silu(c)
    o_ref[...] = jnp.dot(cond, w_ref[...], preferred_element_type=F32,
                         precision=lax.Precision.HIGHEST) + b_ref[...]


def _modulation(c, w_ada, b_ada):
    bsz, d = c.shape
    n = w_ada.shape[1]
    bn = 1024
    return pl.pallas_call(
        _mod_kernel,
        grid=(n // bn,),
        in_specs=[pl.BlockSpec((bsz, d), lambda j: (0, 0)),
                  pl.BlockSpec((d, bn), lambda j: (0, j)),
                  pl.BlockSpec((1, bn), lambda j: (0, j))],
        out_specs=pl.BlockSpec((bsz, bn), lambda j: (0, j)),
        out_shape=jax.ShapeDtypeStruct((bsz, n), F32),
        compiler_params=_params(("arbitrary",)),
        name="adaln_mod",
    )(c, w_ada, b_ada.reshape(1, n))


def _inproj_kernel(x_ref, mod_ref, g_ref, w_ref, wdt_ref, o_ref, dt_ref, h_scr):
    @pl.when(pl.program_id(1) == 0)
    def _():
        sh = mod_ref[0, 0:1, :]
        sc = mod_ref[0, 1:2, :]
        h = (_rms(x_ref[...], g_ref[...]) * (1.0 + sc) + sh).astype(BF16)
        h_scr[...] = h
        dt_ref[...] = jnp.dot(h, wdt_ref[...], preferred_element_type=F32)

    o_ref[...] = jnp.dot(h_scr[...], w_ref[...], preferred_element_type=F32).astype(BF16)


def _in_projection(x2d, mod, g_mix, w_main, w_dt, seq):
    t, d = x2d.shape
    n = w_main.shape[1]
    tm = min(1024, seq)
    tn = 2048
    per_b = seq // tm
    return pl.pallas_call(
        _inproj_kernel,
        grid=(t // tm, n // tn),
        in_specs=[pl.BlockSpec((tm, d), lambda i, j: (i, 0)),
                  pl.BlockSpec((1, N_MOD, d), lambda i, j: (i // per_b, 0, 0)),
                  pl.BlockSpec((1, d), lambda i, j: (0, 0)),
                  pl.BlockSpec((d, tn), lambda i, j: (0, j)),
                  pl.BlockSpec((d, LANES), lambda i, j: (0, 0))],
        out_specs=[pl.BlockSpec((tm, tn), lambda i, j: (i, j)),
                   pl.BlockSpec((tm, LANES), lambda i, j: (i, 0))],
        out_shape=[jax.ShapeDtypeStruct((t, n), BF16),
                   jax.ShapeDtypeStruct((t, LANES), F32)],
        scratch_shapes=[pltpu.VMEM((tm, d), BF16)],
        compiler_params=_params(("arbitrary", "arbitrary")),
        name="in_proj",
    )(x2d, mod, g_mix.reshape(1, d), w_main, w_dt)


def _split3(v):
    hi = v.astype(BF16)
    r1 = v - hi.astype(F32)
    mid = r1.astype(BF16)
    lo = (r1 - mid.astype(F32)).astype(BF16)
    return hi, mid, lo


def _mixer_kernel(proj_ref, dt_ref, x_ref, mod_ref, wsc_ref, wcv_ref, bcv_ref, dtb_ref, alog_ref,
                  dskip_ref, gn_ref, e_ref, wa_ref, wb_ref, wo_ref, o_ref,
                  s_scr, hx_scr, hu_scr, extx, extu):
    L = SSM_CHUNK

    @pl.when(pl.program_id(1) == 0)
    def _():
        s_scr[...] = jnp.zeros_like(s_scr)
        hx_scr[...] = jnp.zeros_like(hx_scr)
        hu_scr[...] = jnp.zeros_like(hu_scr)

    def col(lo, width):
        return proj_ref[:, lo:lo + width].astype(F32)

    u = col(P_SB, D_MODEL) * col(P_SX, D_MODEL)
    extu[0:HALO, :] = hu_scr[...]
    extu[HALO:HALO + L, :] = u
    conv_a = wsc_ref[CONV_K - 1:CONV_K, :] * u
    for k in range(CONV_K - 1):
        back = CONV_K - 1 - k
        conv_a = conv_a + wsc_ref[k:k + 1, :] * extu[HALO - back:HALO - back + L, :]
    hu_scr[...] = extu[L:L + HALO, :]
    ya_in = (col(P_SC, D_MODEL) * conv_a).astype(BF16)
    y_a = jnp.dot(ya_in, wa_ref[...], preferred_element_type=F32)

    xbc = col(P_XBC, D_XBC)
    extx[0:HALO, :] = hx_scr[...]
    extx[HALO:HALO + L, :] = xbc
    cv = wcv_ref[SSM_CONV_K - 1:SSM_CONV_K, :] * xbc + bcv_ref[...]
    for k in range(SSM_CONV_K - 1):
        back = SSM_CONV_K - 1 - k
        cv = cv + wcv_ref[k:k + 1, :] * extx[HALO - back:HALO - back + L, :]
    hx_scr[...] = extx[L:L + HALO, :]
    act = _silu(cv)
    xs = act[:, :D_INNER]
    bs = act[:, D_INNER:D_INNER + SSM_GROUPS * SSM_STATE]
    cs = act[:, D_INNER + SSM_GROUPS * SSM_STATE:]

    dt = jax.nn.softplus(dt_ref[...] + dtb_ref[...])
    a = -jnp.exp(alog_ref[...])
    da = dt * a
    row = lax.broadcasted_iota(jnp.int32, (L, L), 0)
    cidx = lax.broadcasted_iota(jnp.int32, (L, L), 1)
    causal = cidx <= row
    tri = jnp.where(causal, 1.0, 0.0).astype(BF16)
    a_cum = sum(jnp.dot(tri, p, preferred_element_type=F32) for p in _split3(da))
    a_cum_t = a_cum.T
    ea = jnp.exp(a_cum)
    de = jnp.exp(a_cum[L - 1:L, :] - a_cum)
    e_mat = e_ref[...]
    dt_e = jnp.dot(dt.astype(BF16), e_mat, preferred_element_type=F32)
    w_e = jnp.dot((dt * de).astype(BF16), e_mat, preferred_element_type=F32)
    ea_e = jnp.dot(ea.astype(BF16), e_mat, preferred_element_type=F32)
    xdt = (xs * dt_e).astype(BF16)
    xw = (xs * w_e).astype(BF16)
    bs_t = bs.T
    lane = lax.broadcasted_iota(jnp.int32, (L, 2 * SSM_HEAD_DIM), 1)
    lo_half = lane < SSM_HEAD_DIM

    y_groups = []
    for g in range(SSM_GROUPS):
        cg = cs[:, g * SSM_STATE:(g + 1) * SSM_STATE].astype(BF16)
        bg = bs[:, g * SSM_STATE:(g + 1) * SSM_STATE].astype(BF16)
        bg_t = bs_t[g * SSM_STATE:(g + 1) * SSM_STATE, :].astype(BF16)
        cb = lax.dot_general(cg, bg, (((1,), (1,)), ((), ())), preferred_element_type=F32)
        s_old = s_scr[g]
        gsl = slice(g * GROUP_WIDTH, (g + 1) * GROUP_WIDTH)
        y_off = jnp.dot(cg, s_old.astype(BF16), preferred_element_type=F32) * ea_e[:, gsl]
        pieces = []
        for p in range(HEADS_PER_GROUP // 2):
            acc = None
            xpair = xdt[:, g * GROUP_WIDTH + p * 2 * SSM_HEAD_DIM:g * GROUP_WIDTH + (p + 1) * 2 * SSM_HEAD_DIM]
            for half in range(2):
                h = g * HEADS_PER_GROUP + 2 * p + half
                seg = a_cum[:, h:h + 1] - a_cum_t[h:h + 1, :]
                decay = jnp.exp(jnp.where(causal, seg, -jnp.inf))
                scores = (cb * decay).astype(BF16)
                keep = lo_half if half == 0 else jnp.logical_not(lo_half)
                part = jnp.dot(scores, jnp.where(keep, xpair, jnp.zeros_like(xpair)),
                               preferred_element_type=F32)
                acc = part if acc is None else acc + part
            pieces.append(acc)
        y_groups.append(jnp.concatenate(pieces, axis=1) + y_off)
        s_new = s_old * ea_e[L - 1:L, gsl] + jnp.dot(bg_t, xw[:, gsl], preferred_element_type=F32)
        s_scr[g] = s_new
    y = jnp.concatenate(y_groups, axis=1) + dskip_ref[...] * xs

    yf = y * _silu(col(P_Z, D_INNER))
    normed = []
    for g in range(SSM_GROUPS):
        blk = yf[:, g * GROUP_WIDTH:(g + 1) * GROUP_WIDTH]
        normed.append(blk * lax.rsqrt(jnp.mean(blk * blk, axis=-1, keepdims=True) + EPS))
    yn = (jnp.concatenate(normed, axis=1) * gn_ref[...]).astype(BF16)
    y_b = jnp.dot(yn, wb_ref[...], preferred_element_type=F32)
    m = jax.nn.sigmoid(col(P_GA, D_MODEL)) * y_a + jax.nn.sigmoid(col(P_GB, D_MODEL)) * y_b
    mix = jnp.dot(m.astype(BF16), wo_ref[...], preferred_element_type=F32)
    o_ref[...] = x_ref[...] + mod_ref[0, 2:3, :] * mix


def _token_mixer(proj, dt_raw, x2d, mod, w_sconv, w_ssm_conv, b_ssm_conv, dt_bias, a_log, d_skip,
                 g_ssm_norm, w_a, w_b, w_o, bsz, seq):
    t, d = x2d.shape
    L = SSM_CHUNK
    nc = seq // L
    pad_h = LANES - SSM_HEADS
    dtb = jnp.pad(dt_bias, (0, pad_h)).reshape(1, LANES)
    alog = jnp.pad(a_log, (0, pad_h)).reshape(1, LANES)
    dskip_e = jnp.repeat(d_skip, SSM_HEAD_DIM).reshape(1, D_INNER)
    expand = (jnp.arange(D_INNER)[None, :] // SSM_HEAD_DIM == jnp.arange(LANES)[:, None]).astype(BF16)
    full = lambda shape: pl.BlockSpec(shape, lambda b, c: (0,) * len(shape))
    return pl.pallas_call(
        _mixer_kernel,
        grid=(bsz, nc),
        in_specs=[pl.BlockSpec((L, P_TOTAL), lambda b, c: (b * nc + c, 0)),
                  pl.BlockSpec((L, LANES), lambda b, c: (b * nc + c, 0)),
                  pl.BlockSpec((L, d), lambda b, c: (b * nc + c, 0)),
                  pl.BlockSpec((1, N_MOD, d), lambda b, c: (b, 0, 0)),
                  full((CONV_K, D_MODEL)), full((SSM_CONV_K, D_XBC)), full((1, D_XBC)),
                  full((1, LANES)), full((1, LANES)), full((1, D_INNER)), full((1, D_INNER)),
                  full((LANES, D_INNER)),
                  full((D_MODEL, D_MODEL)), full((D_INNER, D_MODEL)), full((D_MODEL, D_MODEL))],
        out_specs=pl.BlockSpec((L, d), lambda b, c: (b * nc + c, 0)),
        out_shape=jax.ShapeDtypeStruct((t, d), F32),
        scratch_shapes=[pltpu.VMEM((SSM_GROUPS, SSM_STATE, GROUP_WIDTH), F32),
                        pltpu.VMEM((HALO, D_XBC), F32),
                        pltpu.VMEM((HALO, D_MODEL), F32),
                        pltpu.VMEM((L + HALO, D_XBC), F32),
                        pltpu.VMEM((L + HALO, D_MODEL), F32)],
        compiler_params=_params(("arbitrary", "arbitrary")),
        name="token_mixer",
    )(proj, dt_raw, x2d, mod, w_sconv, w_ssm_conv, b_ssm_conv.reshape(1, D_XBC), dtb, alog, dskip_e,
      g_ssm_norm.reshape(1, D_INNER), expand, w_a, w_b, w_o)


def _ffn_input(x, mod_ref, g_ref):
    return _rms(x, g_ref[...]) * (1.0 + mod_ref[0, 4:5, :]) + mod_ref[0, 3:4, :]


def _router_kernel(x_ref, mod_ref, g_ref, wr_ref, br_ref, pos_ref, post_ref, w_ref, cnt_ref):
    tm = x_ref.shape[0]
    h = _ffn_input(x_ref[...], mod_ref, g_ref)
    logits = jnp.dot(h, wr_ref[...], preferred_element_type=F32,
                     precision=lax.Precision.HIGHEST) + br_ref[...]
    lane = lax.broadcasted_iota(jnp.int32, (tm, LANES), 1)
    work = jnp.where(lane < N_EXPERTS, logits, -jnp.inf)
    vals, hots = [], []
    for _ in range(TOP_K):
        v = jnp.max(work, axis=-1, keepdims=True)
        i = jnp.min(jnp.where(work == v, lane, LANES), axis=-1, keepdims=True)
        hot = lane == i
        vals.append(v)
        hots.append(hot)
        work = jnp.where(hot, -jnp.inf, work)
    exps = [jnp.exp(v - vals[0]) for v in vals]
    denom = exps[0] + exps[1] + exps[2] + exps[3]

    onehot = jnp.zeros((tm, LANES), F32)
    for hot in hots:
        onehot = onehot + jnp.where(hot, 1.0, 0.0)
    r = lax.broadcasted_iota(jnp.int32, (tm, tm), 0)
    c = lax.broadcasted_iota(jnp.int32, (tm, tm), 1)
    strict = jnp.where(c < r, 1.0, 0.0).astype(BF16)
    before = jnp.dot(strict, onehot.astype(BF16), preferred_element_type=F32)
    cnt = jnp.sum(onehot, axis=0, keepdims=True).astype(jnp.int32)
    seg_units = (cnt + (SEG_ALIGN - 1)) // SEG_ALIGN
    er = lax.broadcasted_iota(jnp.int32, (LANES, LANES), 0)
    ec = lax.broadcasted_iota(jnp.int32, (LANES, LANES), 1)
    upper = jnp.where(er < ec, 1.0, 0.0).astype(BF16)
    units8 = jnp.broadcast_to(seg_units.astype(F32), (8, LANES)).astype(BF16)
    seg_off = jnp.dot(units8, upper, preferred_element_type=F32)[0:1, :] * float(SEG_ALIGN)
    where_to = before + seg_off
    packed = jnp.zeros((tm, LANES), F32)
    for k in range(TOP_K):
        pos = jnp.sum(jnp.where(hots[k], where_to, 0.0), axis=-1, keepdims=True)
        pos_ref[:, k:k + 1] = pos.astype(jnp.int32)
        w_ref[:, k:k + 1] = exps[k] / denom
        packed = jnp.where(lane == k, pos, packed)
    post_ref[...] = packed.T[0:8, :].astype(jnp.int32)
    cnt_ref[0] = cnt


def _router(x1, mod, g_ffn, w_router, b_router, seq):
    t, d = x1.shape
    tm = MOE_TILE
    per_b = seq // tm
    wr = jnp.pad(w_router, ((0, 0), (0, LANES - N_EXPERTS)))
    br = jnp.pad(b_router, (0, LANES - N_EXPERTS)).reshape(1, LANES)
    slot_spec = pl.BlockSpec((tm, TOP_K), lambda i: (i, 0))
    return pl.pallas_call(
        _router_kernel,
        grid=(t // tm,),
        in_specs=[pl.BlockSpec((tm, d), lambda i: (i, 0)),
                  pl.BlockSpec((1, N_MOD, d), lambda i: (i // per_b, 0, 0)),
                  pl.BlockSpec((1, d), lambda i: (0, 0)),
                  pl.BlockSpec((d, LANES), lambda i: (0, 0)),
                  pl.BlockSpec((1, LANES), lambda i: (0, 0))],
        out_specs=[slot_spec, pl.BlockSpec((8, tm), lambda i: (0, i)), slot_spec,
                   pl.BlockSpec((1, 1, LANES), lambda i: (i, 0, 0))],
        out_shape=[jax.ShapeDtypeStruct((t, TOP_K), jnp.int32),
                   jax.ShapeDtypeStruct((8, t), jnp.int32),
                   jax.ShapeDtypeStruct((t, TOP_K), F32),
                   jax.ShapeDtypeStruct((t // tm, 1, LANES), jnp.int32)],
        compiler_params=_params(("arbitrary",)),
        name="router",
    )(x1, mod, g_ffn.reshape(1, d), wr, br)


def _segment_copies(tile, len_ref, loc_ref, glob_ref, make_copy, wait):
    def per_expert(e, carry):
        j = tile * N_EXPERTS + e
        ln, lo, go = len_ref[j], loc_ref[j], glob_ref[j]
        for b in SEG_BITS:
            above = ln & ~(2 * b - 1)

            @pl.when((ln & b) != 0)
            def _():
                cp = make_copy(pl.multiple_of(lo + above, SEG_ALIGN), pl.multiple_of(go + above, SEG_ALIGN), b)
                cp.wait() if wait else cp.start()
        return carry

    lax.fori_loop(0, N_EXPERTS, per_expert, 0)


def _dispatch_kernel(len_ref, loc_ref, glob_ref, tail_ref, post_ref, x_ref, mod_ref, g_ref, xs_ref,
                     sort_scr, zero_scr, sem):
    i = pl.program_id(0)
    tm = x_ref.shape[0]
    r_max = sort_scr.shape[0]

    @pl.when(i == 0)
    def _():
        zero_scr[...] = jnp.zeros_like(zero_scr)
        for wait in (False, True):
            for e in range(N_EXPERTS):
                @pl.when(tail_ref[e] >= 0)
                def _():
                    cp = pltpu.make_async_copy(
                        zero_scr, xs_ref.at[pl.ds(pl.multiple_of(tail_ref[e], SEG_ALIGN), MOE_BLOCK), :], sem)
                    cp.wait() if wait else cp.start()

    h = _ffn_input(x_ref[...], mod_ref, g_ref).astype(BF16)
    rows = lax.broadcasted_iota(jnp.int32, (r_max, tm), 0)
    sel = rows == post_ref[0:1, :]
    for k in range(1, TOP_K):
        sel = jnp.logical_or(sel, rows == post_ref[k:k + 1, :])
    perm = jnp.where(sel, 1.0, 0.0).astype(BF16)
    sort_scr[...] = jnp.dot(perm, h, preferred_element_type=F32).astype(BF16)

    def make_copy(lo, go, n):
        return pltpu.make_async_copy(sort_scr.at[pl.ds(lo, n), :], xs_ref.at[pl.ds(go, n), :], sem)

    _segment_copies(i, len_ref, loc_ref, glob_ref, make_copy, wait=False)
    _segment_copies(i, len_ref, loc_ref, glob_ref, make_copy, wait=True)


def _dispatch(seg_len, seg_loc, seg_glob, tail, pos_t, x1, mod, g_ffn, r_tot, seq):
    t, d = x1.shape
    tm = MOE_TILE
    per_b = seq // tm
    return pl.pallas_call(
        _dispatch_kernel,
        grid_spec=pltpu.PrefetchScalarGridSpec(
            num_scalar_prefetch=4, grid=(t // tm,),
            in_specs=[pl.BlockSpec((8, tm), lambda i, *_: (0, i)),
                      pl.BlockSpec((tm, d), lambda i, *_: (i, 0)),
                      pl.BlockSpec((1, N_MOD, d), lambda i, *_: (i // per_b, 0, 0)),
                      pl.BlockSpec((1, d), lambda i, *_: (0, 0))],
            out_specs=pl.BlockSpec(memory_space=pl.ANY),
            scratch_shapes=[pltpu.VMEM((SORT_ROWS, d), BF16), pltpu.VMEM((MOE_BLOCK, d), BF16),
                            pltpu.SemaphoreType.DMA]),
        out_shape=jax.ShapeDtypeStruct((r_tot, d), BF16),
        compiler_params=_params(("arbitrary",)),
        name="dispatch",
    )(seg_len, seg_loc, seg_glob, tail, pos_t, x1, mod, g_ffn.reshape(1, d))


def _expert_kernel(be_ref, nu_ref, x_ref, wgu_ref, bgu_ref, wd_ref, bd_ref, o_ref):
    @pl.when(pl.program_id(0) < nu_ref[0])
    def _():
        gu = jnp.dot(x_ref[...], wgu_ref[0], preferred_element_type=F32) + bgu_ref[0]
        gate = jnp.minimum(gu[:, :D_FF], SWIGLU_LIMIT)
        up = jnp.clip(gu[:, D_FF:], -SWIGLU_LIMIT, SWIGLU_LIMIT)
        act = (up + 1.0) * gate * jax.nn.sigmoid(SWIGLU_ALPHA * gate)
        y = jnp.dot(act.astype(BF16), wd_ref[0], preferred_element_type=F32) + bd_ref[0]
        o_ref[...] = y.astype(BF16)


def _expert_blocks(block_e, n_used, xs, w_gu, b_gu, w_down, b_down):
    r_tot, d = xs.shape
    n_blocks = r_tot // MOE_BLOCK
    row_map = lambda i, be, nu: (jnp.minimum(i, nu[0] - 1), 0)
    e_map = lambda i, be, nu: (be[i], 0, 0)
    return pl.pallas_call(
        _expert_kernel,
        grid_spec=pltpu.PrefetchScalarGridSpec(
            num_scalar_prefetch=2, grid=(n_blocks,),
            in_specs=[pl.BlockSpec((MOE_BLOCK, d), row_map),
                      pl.BlockSpec((1, d, 2 * D_FF), e_map),
                      pl.BlockSpec((1, 1, 2 * D_FF), e_map),
                      pl.BlockSpec((1, D_FF, d), e_map),
                      pl.BlockSpec((1, 1, d), e_map)],
            out_specs=pl.BlockSpec((MOE_BLOCK, d), row_map)),
        out_shape=jax.ShapeDtypeStruct((r_tot, d), BF16),
        compiler_params=_params(("arbitrary",)),
        name="expert_blocks",
    )(block_e, n_used, xs, w_gu, b_gu.reshape(N_EXPERTS, 1, 2 * D_FF), w_down,
      b_down.reshape(N_EXPERTS, 1, d))


def _combine_kernel(len_ref, loc_ref, glob_ref, pos_ref, w_ref, x_ref, mod_ref, g_ref, y_ref, o_ref,
                    sort_scr, sem, *, final_norm):
    i = pl.program_id(0)
    tm = x_ref.shape[0]
    r_max = sort_scr.shape[0]
    sort_scr[...] = jnp.zeros_like(sort_scr)

    def make_copy(lo, go, n):
        return pltpu.make_async_copy(y_ref.at[pl.ds(go, n), :], sort_scr.at[pl.ds(lo, n), :], sem)

    _segment_copies(i, len_ref, loc_ref, glob_ref, make_copy, wait=False)
    _segment_copies(i, len_ref, loc_ref, glob_ref, make_copy, wait=True)

    cols = lax.broadcasted_iota(jnp.int32, (tm, r_max), 1)
    w = w_ref[...]
    pos = pos_ref[...]
    gather_w = jnp.where(cols == pos[:, 0:1], w[:, 0:1], 0.0)
    for k in range(1, TOP_K):
        gather_w = gather_w + jnp.where(cols == pos[:, k:k + 1], w[:, k:k + 1], 0.0)
    moe = jnp.dot(gather_w.astype(BF16), sort_scr[...], preferred_element_type=F32)
    x2 = x_ref[...] + mod_ref[0, 5:6, :] * moe
    o_ref[...] = _rms(x2, g_ref[...]) if final_norm else x2


def _combine(seg_len, seg_loc, seg_glob, pos, top_w, x1, mod, g_final, y, seq, final_norm):
    t, d = x1.shape
    tm = MOE_TILE
    per_b = seq // tm
    slot_spec = pl.BlockSpec((tm, TOP_K), lambda i, *_: (i, 0))
    return pl.pallas_call(
        functools.partial(_combine_kernel, final_norm=final_norm),
        grid_spec=pltpu.PrefetchScalarGridSpec(
            num_scalar_prefetch=3, grid=(t // tm,),
            in_specs=[slot_spec, slot_spec,
                      pl.BlockSpec((tm, d), lambda i, *_: (i, 0)),
                      pl.BlockSpec((1, N_MOD, d), lambda i, *_: (i // per_b, 0, 0)),
                      pl.BlockSpec((1, d), lambda i, *_: (0, 0)),
                      pl.BlockSpec(memory_space=pl.ANY)],
            out_specs=pl.BlockSpec((tm, d), lambda i, *_: (i, 0)),
            scratch_shapes=[pltpu.VMEM((SORT_ROWS, d), BF16), pltpu.SemaphoreType.DMA]),
        out_shape=jax.ShapeDtypeStruct((t, d), F32),
        compiler_params=_params(("arbitrary",)),
        name="combine",
    )(seg_len, seg_loc, seg_glob, pos, top_w, x1, mod, g_final.reshape(1, d), y)


def _regroup_w_in(w_in):
    o_sb, o_sc, o_sx = 0, D_MODEL, 2 * D_MODEL
    o_z = 3 * D_MODEL
    o_xbc = o_z + D_INNER
    o_dt = o_xbc + D_XBC
    o_ga = o_dt + SSM_HEADS
    o_gb = o_ga + D_MODEL
    main = jnp.concatenate([w_in[:, o_z:o_xbc], w_in[:, o_xbc:o_dt], w_in[:, o_sb:o_sc], w_in[:, o_sc:o_sx],
                            w_in[:, o_sx:o_z], w_in[:, o_ga:o_gb], w_in[:, o_gb:o_gb + D_MODEL]], axis=1)
    w_dt = jnp.pad(w_in[:, o_dt:o_ga], ((0, 0), (0, LANES - SSM_HEADS)))
    return main.astype(BF16), w_dt.astype(BF16)


def _segment_tables(counts, n_blocks):
    seg_len = (counts + SEG_ALIGN - 1) // SEG_ALIGN * SEG_ALIGN
    seg_loc = jnp.cumsum(seg_len, axis=1) - seg_len
    used = jnp.sum(seg_len, axis=0)
    padded = (used + MOE_BLOCK - 1) // MOE_BLOCK * MOE_BLOCK
    pend = jnp.cumsum(padded)
    pstart = pend - padded
    seg_glob = pstart[None, :] + jnp.cumsum(seg_len, axis=0) - seg_len
    tail = jnp.where(padded > 0, pend - MOE_BLOCK, -1)
    first_row = jnp.arange(n_blocks, dtype=jnp.int32) * MOE_BLOCK
    block_e = jnp.sum((pend[None, :] <= first_row[:, None]).astype(jnp.int32), axis=1)
    block_e = jnp.minimum(block_e, N_EXPERTS - 1)
    n_used = pend[-1:] // MOE_BLOCK
    i32 = lambda a: a.astype(jnp.int32)
    return (i32(seg_len).reshape(-1), i32(seg_loc).reshape(-1), i32(seg_glob).reshape(-1), i32(tail),
            i32(block_e), i32(n_used))


def kernel(x, c, w_ada, b_ada, g_mix, w_in, w_sconv, w_sconv_out, w_ssm_conv, b_ssm_conv, dt_bias, a_log,
           d_skip, g_ssm_norm, w_ssm_out, w_o, g_ffn, w_router, b_router, w_gu, b_gu, w_down, b_down,
           g_final):
    bsz, seq, d = x.shape
    depth = w_ada.shape[0]
    t = bsz * seq
    assert seq % MOE_TILE == 0 and seq % SSM_CHUNK == 0
    n_tiles = t // MOE_TILE
    worst_rows = t * TOP_K + n_tiles * N_EXPERTS * (SEG_ALIGN - 1) + N_EXPERTS * (MOE_BLOCK - 1)
    n_blocks = -(-worst_rows // MOE_BLOCK)
    r_tot = n_blocks * MOE_BLOCK
    xf = x.reshape(t, d)
    for l in range(depth):
        mod = _modulation(c, w_ada[l], b_ada[l]).reshape(bsz, N_MOD, d)
        w_main, w_dt = _regroup_w_in(w_in[l])
        proj, dt_raw = _in_projection(xf, mod, g_mix[l], w_main, w_dt, seq)
        x1 = _token_mixer(proj, dt_raw, xf, mod, w_sconv[l], w_ssm_conv[l], b_ssm_conv[l], dt_bias[l],
                          a_log[l], d_skip[l], g_ssm_norm[l], w_sconv_out[l].astype(BF16),
                          w_ssm_out[l].astype(BF16), w_o[l].astype(BF16), bsz, seq)
        pos, pos_t, top_w, counts = _router(x1, mod, g_ffn[l], w_router[l], b_router[l], seq)
        seg_len, seg_loc, seg_glob, tail, block_e, n_used = _segment_tables(
            counts[:, 0, :N_EXPERTS], n_blocks)
        xs = _dispatch(seg_len, seg_loc, seg_glob, tail, pos_t, x1, mod, g_ffn[l], r_tot, seq)
        ys = _expert_blocks(block_e, n_used, xs, w_gu[l].astype(BF16), b_gu[l], w_down[l].astype(BF16),
                            b_down[l])
        xf = _combine(seg_len, seg_loc, seg_glob, pos, top_w, x1, mod, g_final, ys, seq,
                      final_norm=(l == depth - 1))
    return xf.reshape(bsz, seq, d)
```

```python
import functools

import jax
import jax.numpy as jnp
from jax import lax
from jax.experimental import pallas as pl
from jax.experimental.pallas import tpu as pltpu

F32 = jnp.float32
BF16 = jnp.bfloat16

D_MODEL = 1024
CONV_K = 3
D_INNER = 2048
SSM_HEAD_DIM = 64
SSM_HEADS = 32
SSM_GROUPS = 4
SSM_STATE = 128
SSM_CONV_K = 4
SSM_CHUNK = 256
D_XBC = D_INNER + 2 * SSM_GROUPS * SSM_STATE
N_EXPERTS = 32
TOP_K = 4
D_FF = D_MODEL
SWIGLU_LIMIT = 7.0
SWIGLU_ALPHA = 1.702
EPS = 1e-6
N_MOD = 6

HEADS_PER_GROUP = SSM_HEADS // SSM_GROUPS
GROUP_WIDTH = D_INNER // SSM_GROUPS
LANES = 128
HALO = 8

MOE_BLOCK = 512
MOE_TILE = 512
SEG_ALIGN = 16
MXU_DIM = 256
SORT_ROWS = -(-(MOE_TILE * TOP_K + N_EXPERTS * (SEG_ALIGN - 1)) // MXU_DIM) * MXU_DIM
SEG_BITS = tuple(SEG_ALIGN << b for b in reversed(range((MOE_TILE // SEG_ALIGN).bit_length())))

P_Z = 0
P_XBC = P_Z + D_INNER
P_SB = P_XBC + D_XBC
P_SC = P_SB + D_MODEL
P_SX = P_SC + D_MODEL
P_GA = P_SX + D_MODEL
P_GB = P_GA + D_MODEL
P_TOTAL = P_GB + D_MODEL

VMEM_LIMIT = 56 * 1024 * 1024


def _params(sem):
    return pltpu.CompilerParams(dimension_semantics=sem, vmem_limit_bytes=VMEM_LIMIT)


def _rms(x, g):
    return x * lax.rsqrt(jnp.mean(x * x, axis=-1, keepdims=True) + EPS) * g


def _silu(x):
    return x * jax.nn.sigmoid(x)


def _mod_kernel(c_ref, w_ref, b_ref, o_ref):
    c = c_ref[...]
    cond = _silu(c)
    o_ref[...] = jnp.dot(cond, w_ref[...], preferred_element_type=F32,
                         precision=lax.Precision.HIGHEST) + b_ref[...]


def _modulation(c, w_ada, b_ada):
    bsz, d = c.shape
    n = w_ada.shape[1]
    bn = 1024
    return pl.pallas_call(
        _mod_kernel,
        grid=(n // bn,),
        in_specs=[pl.BlockSpec((bsz, d), lambda j: (0, 0)),
                  pl.BlockSpec((d, bn), lambda j: (0, j)),
                  pl.BlockSpec((1, bn), lambda j: (0, j))],
        out_specs=pl.BlockSpec((bsz, bn), lambda j: (0, j)),
        out_shape=jax.ShapeDtypeStruct((bsz, n), F32),
        compiler_params=_params(("arbitrary",)),
        name="adaln_mod",
    )(c, w_ada, b_ada.reshape(1, n))


def _inproj_kernel(x_ref, mod_ref, g_ref, w_ref, wdt_ref, o_ref, dt_ref):
    sh = mod_ref[0, 0:1, :]
    sc = mod_ref[0, 1:2, :]
    h = (_rms(x_ref[...], g_ref[...]) * (1.0 + sc) + sh).astype(BF16)
    dt_ref[...] = jnp.dot(h, wdt_ref[...], preferred_element_type=F32)
    o_ref[...] = jnp.dot(h, w_ref[...], preferred_element_type=F32).astype(BF16)


def _in_projection(x2d, mod, g_mix, w_main, w_dt, seq):
    t, d = x2d.shape
    n = w_main.shape[1]
    tm = min(512, seq)
    per_b = seq // tm
    return pl.pallas_call(
        _inproj_kernel,
        grid=(t // tm,),
        in_specs=[pl.BlockSpec((tm, d), lambda i: (i, 0)),
                  pl.BlockSpec((1, N_MOD, d), lambda i: (i // per_b, 0, 0)),
                  pl.BlockSpec((1, d), lambda i: (0, 0)),
                  pl.BlockSpec((d, n), lambda i: (0, 0)),
                  pl.BlockSpec((d, LANES), lambda i: (0, 0))],
        out_specs=[pl.BlockSpec((tm, n), lambda i: (i, 0)),
                   pl.BlockSpec((tm, LANES), lambda i: (i, 0))],
        out_shape=[jax.ShapeDtypeStruct((t, n), BF16),
                   jax.ShapeDtypeStruct((t, LANES), F32)],
        compiler_params=_params(("arbitrary",)),
        name="in_proj",
    )(x2d, mod, g_mix.reshape(1, d), w_main, w_dt)


def _split3(v):
    hi = v.astype(BF16)
    r1 = v - hi.astype(F32)
    mid = r1.astype(BF16)
    lo = (r1 - mid.astype(F32)).astype(BF16)
    return hi, mid, lo


def _mixer_kernel(proj_ref, dt_ref, x_ref, mod_ref, wsc_ref, wcv_ref, bcv_ref, dtb_ref, alog_ref,
                  dskip_ref, gn_ref, e_ref, wa_ref, wb_ref, wo_ref, o_ref,
                  s_scr, hx_scr, hu_scr, extx, extu):
    L = SSM_CHUNK

    @pl.when(pl.program_id(1) == 0)
    def _():
        s_scr[...] = jnp.zeros_like(s_scr)
        hx_scr[...] = jnp.zeros_like(hx_scr)
        hu_scr[...] = jnp.zeros_like(hu_scr)

    def col(lo, width):
        return proj_ref[:, lo:lo + width].astype(F32)

    u = col(P_SB, D_MODEL) * col(P_SX, D_MODEL)
    extu[0:HALO, :] = hu_scr[...]
    extu[HALO:HALO + L, :] = u
    conv_a = wsc_ref[CONV_K - 1:CONV_K, :] * u
    for k in range(CONV_K - 1):
        back = CONV_K - 1 - k
        conv_a = conv_a + wsc_ref[k:k + 1, :] * extu[HALO - back:HALO - back + L, :]
    hu_scr[...] = extu[L:L + HALO, :]
    ya_in = (col(P_SC, D_MODEL) * conv_a).astype(BF16)
    y_a = jnp.dot(ya_in, wa_ref[...], preferred_element_type=F32)

    xbc = col(P_XBC, D_XBC)
    extx[0:HALO, :] = hx_scr[...]
    extx[HALO:HALO + L, :] = xbc
    cv = wcv_ref[SSM_CONV_K - 1:SSM_CONV_K, :] * xbc + bcv_ref[...]
    for k in range(SSM_CONV_K - 1):
        back = SSM_CONV_K - 1 - k
        cv = cv + wcv_ref[k:k + 1, :] * extx[HALO - back:HALO - back + L, :]
    hx_scr[...] = extx[L:L + HALO, :]
    act = _silu(cv)
    xs = act[:, :D_INNER]
    bs = act[:, D_INNER:D_INNER + SSM_GROUPS * SSM_STATE]
    cs = act[:, D_INNER + SSM_GROUPS * SSM_STATE:]

    dt = jax.nn.softplus(dt_ref[...] + dtb_ref[...])
    a = -jnp.exp(alog_ref[...])
    da = dt * a
    row = lax.broadcasted_iota(jnp.int32, (L, L), 0)
    cidx = lax.broadcasted_iota(jnp.int32, (L, L), 1)
    causal = cidx <= row
    tri = jnp.where(causal, 1.0, 0.0).astype(BF16)
    a_cum = sum(jnp.dot(tri, p, preferred_element_type=F32) for p in _split3(da))
    a_cum_t = a_cum.T
    ea = jnp.exp(a_cum)
    de = jnp.exp(a_cum[L - 1:L, :] - a_cum)
    e_mat = e_ref[...]
    dt_e = jnp.dot(dt.astype(BF16), e_mat, preferred_element_type=F32)
    w_e = jnp.dot((dt * de).astype(BF16), e_mat, preferred_element_type=F32)
    ea_e = jnp.dot(ea.astype(BF16), e_mat, preferred_element_type=F32)
    xdt = (xs * dt_e).astype(BF16)
    xw = (xs * w_e).astype(BF16)
    bs_t = bs.T
    lane = lax.broadcasted_iota(jnp.int32, (L, 2 * SSM_HEAD_DIM), 1)
    lo_half = lane < SSM_HEAD_DIM

    y_groups = []
    for g in range(SSM_GROUPS):
        cg = cs[:, g * SSM_STATE:(g + 1) * SSM_STATE].astype(BF16)
        bg = bs[:, g * SSM_STATE:(g + 1) * SSM_STATE].astype(BF16)
        bg_t = bs_t[g * SSM_STATE:(g + 1) * SSM_STATE, :].astype(BF16)
        cb = lax.dot_general(cg, bg, (((1,), (1,)), ((), ())), preferred_element_type=F32)
        s_old = s_scr[g]
        gsl = slice(g * GROUP_WIDTH, (g + 1) * GROUP_WIDTH)
        y_off = jnp.dot(cg, s_old.astype(BF16), preferred_element_type=F32) * ea_e[:, gsl]
        pieces = []
        for p in range(HEADS_PER_GROUP // 2):
            acc = None
            xpair = xdt[:, g * GROUP_WIDTH + p * 2 * SSM_HEAD_DIM:g * GROUP_WIDTH + (p + 1) * 2 * SSM_HEAD_DIM]
            for half in range(2):
                h = g * HEADS_PER_GROUP + 2 * p + half
                seg = a_cum[:, h:h + 1] - a_cum_t[h:h + 1, :]
                decay = jnp.exp(jnp.where(causal, seg, -jnp.inf))
                scores = (cb * decay).astype(BF16)
                keep = lo_half if half == 0 else jnp.logical_not(lo_half)
                part = jnp.dot(scores, jnp.where(keep, xpair, jnp.zeros_like(xpair)),
                               preferred_element_type=F32)
                acc = part if acc is None else acc + part
            pieces.append(acc)
        y_groups.append(jnp.concatenate(pieces, axis=1) + y_off)
        s_new = s_old * ea_e[L - 1:L, gsl] + jnp.dot(bg_t, xw[:, gsl], preferred_element_type=F32)
        s_scr[g] = s_new
    y = jnp.concatenate(y_groups, axis=1) + dskip_ref[...] * xs

    yf = y * _silu(col(P_Z, D_INNER))
    normed = []
    for g in range(SSM_GROUPS):
        blk = yf[:, g * GROUP_WIDTH:(g + 1) * GROUP_WIDTH]
        normed.append(blk * lax.rsqrt(jnp.mean(blk * blk, axis=-1, keepdims=True) + EPS))
    yn = (jnp.concatenate(normed, axis=1) * gn_ref[...]).astype(BF16)
    y_b = jnp.dot(yn, wb_ref[...], preferred_element_type=F32)
    m = jax.nn.sigmoid(col(P_GA, D_MODEL)) * y_a + jax.nn.sigmoid(col(P_GB, D_MODEL)) * y_b
    mix = jnp.dot(m.astype(BF16), wo_ref[...], preferred_element_type=F32)
    o_ref[...] = x_ref[...] + mod_ref[0, 2:3, :] * mix


def _token_mixer(proj, dt_raw, x2d, mod, w_sconv, w_ssm_conv, b_ssm_conv, dt_bias, a_log, d_skip,
                 g_ssm_norm, w_a, w_b, w_o, bsz, seq):
    t, d = x2d.shape
    L = SSM_CHUNK
    nc = seq // L
    pad_h = LANES - SSM_HEADS
    dtb = jnp.pad(dt_bias, (0, pad_h)).reshape(1, LANES)
    alog = jnp.pad(a_log, (0, pad_h)).reshape(1, LANES)
    dskip_e = jnp.repeat(d_skip, SSM_HEAD_DIM).reshape(1, D_INNER)
    expand = (jnp.arange(D_INNER)[None, :] // SSM_HEAD_DIM == jnp.arange(LANES)[:, None]).astype(BF16)
    full = lambda shape: pl.BlockSpec(shape, lambda b, c: (0,) * len(shape))
    return pl.pallas_call(
        _mixer_kernel,
        grid=(bsz, nc),
        in_specs=[pl.BlockSpec((L, P_TOTAL), lambda b, c: (b * nc + c, 0)),
                  pl.BlockSpec((L, LANES), lambda b, c: (b * nc + c, 0)),
                  pl.BlockSpec((L, d), lambda b, c: (b * nc + c, 0)),
                  pl.BlockSpec((1, N_MOD, d), lambda b, c: (b, 0, 0)),
                  full((CONV_K, D_MODEL)), full((SSM_CONV_K, D_XBC)), full((1, D_XBC)),
                  full((1, LANES)), full((1, LANES)), full((1, D_INNER)), full((1, D_INNER)),
                  full((LANES, D_INNER)),
                  full((D_MODEL, D_MODEL)), full((D_INNER, D_MODEL)), full((D_MODEL, D_MODEL))],
        out_specs=pl.BlockSpec((L, d), lambda b, c: (b * nc + c, 0)),
        out_shape=jax.ShapeDtypeStruct((t, d), F32),
        scratch_shapes=[pltpu.VMEM((SSM_GROUPS, SSM_STATE, GROUP_WIDTH), F32),
                        pltpu.VMEM((HALO, D_XBC), F32),
                        pltpu.VMEM((HALO, D_MODEL), F32),
                        pltpu.VMEM((L + HALO, D_XBC), F32),
                        pltpu.VMEM((L + HALO, D_MODEL), F32)],
        compiler_params=_params(("arbitrary", "arbitrary")),
        name="token_mixer",
    )(proj, dt_raw, x2d, mod, w_sconv, w_ssm_conv, b_ssm_conv.reshape(1, D_XBC), dtb, alog, dskip_e,
      g_ssm_norm.reshape(1, D_INNER), expand, w_a, w_b, w_o)


def _ffn_input(x, mod_ref, g_ref):
    return _rms(x, g_ref[...]) * (1.0 + mod_ref[0, 4:5, :]) + mod_ref[0, 3:4, :]


def _router_kernel(x_ref, mod_ref, g_ref, wr_ref, br_ref, pos_ref, post_ref, w_ref, cnt_ref):
    tm = x_ref.shape[0]
    h = _ffn_input(x_ref[...], mod_ref, g_ref)
    logits = jnp.dot(h, wr_ref[...], preferred_element_type=F32,
                     precision=lax.Precision.HIGHEST) + br_ref[...]
    lane = lax.broadcasted_iota(jnp.int32, (tm, LANES), 1)
    work = jnp.where(lane < N_EXPERTS, logits, -jnp.inf)
    vals, hots = [], []
    for _ in range(TOP_K):
        v = jnp.max(work, axis=-1, keepdims=True)
        i = jnp.min(jnp.where(work == v, lane, LANES), axis=-1, keepdims=True)
        hot = lane == i
        vals.append(v)
        hots.append(hot)
        work = jnp.where(hot, -jnp.inf, work)
    exps = [jnp.exp(v - vals[0]) for v in vals]
    denom = exps[0] + exps[1] + exps[2] + exps[3]

    onehot = jnp.zeros((tm, LANES), F32)
    for hot in hots:
        onehot = onehot + jnp.where(hot, 1.0, 0.0)
    r = lax.broadcasted_iota(jnp.int32, (tm, tm), 0)
    c = lax.broadcasted_iota(jnp.int32, (tm, tm), 1)
    strict = jnp.where(c < r, 1.0, 0.0).astype(BF16)
    before = jnp.dot(strict, onehot.astype(BF16), preferred_element_type=F32)
    cnt = jnp.sum(onehot, axis=0, keepdims=True).astype(jnp.int32)
    seg_units = (cnt + (SEG_ALIGN - 1)) // SEG_ALIGN
    er = lax.broadcasted_iota(jnp.int32, (LANES, LANES), 0)
    ec = lax.broadcasted_iota(jnp.int32, (LANES, LANES), 1)
    upper = jnp.where(er < ec, 1.0, 0.0).astype(BF16)
    units8 = jnp.broadcast_to(seg_units.astype(F32), (8, LANES)).astype(BF16)
    seg_off = jnp.dot(units8, upper, preferred_element_type=F32)[0:1, :] * float(SEG_ALIGN)
    where_to = before + seg_off
    packed = jnp.zeros((tm, LANES), F32)
    for k in range(TOP_K):
        pos = jnp.sum(jnp.where(hots[k], where_to, 0.0), axis=-1, keepdims=True)
        pos_ref[:, k:k + 1] = pos.astype(jnp.int32)
        w_ref[:, k:k + 1] = exps[k] / denom
        packed = jnp.where(lane == k, pos, packed)
    post_ref[...] = packed.T[0:8, :].astype(jnp.int32)
    cnt_ref[0] = cnt


def _router(x1, mod, g_ffn, w_router, b_router, seq):
    t, d = x1.shape
    tm = MOE_TILE
    per_b = seq // tm
    wr = jnp.pad(w_router, ((0, 0), (0, LANES - N_EXPERTS)))
    br = jnp.pad(b_router, (0, LANES - N_EXPERTS)).reshape(1, LANES)
    slot_spec = pl.BlockSpec((tm, TOP_K), lambda i: (i, 0))
    return pl.pallas_call(
        _router_kernel,
        grid=(t // tm,),
        in_specs=[pl.BlockSpec((tm, d), lambda i: (i, 0)),
                  pl.BlockSpec((1, N_MOD, d), lambda i: (i // per_b, 0, 0)),
                  pl.BlockSpec((1, d), lambda i: (0, 0)),
                  pl.BlockSpec((d, LANES), lambda i: (0, 0)),
                  pl.BlockSpec((1, LANES), lambda i: (0, 0))],
        out_specs=[slot_spec, pl.BlockSpec((8, tm), lambda i: (0, i)), slot_spec,
                   pl.BlockSpec((1, 1, LANES), lambda i: (i, 0, 0))],
        out_shape=[jax.ShapeDtypeStruct((t, TOP_K), jnp.int32),
                   jax.ShapeDtypeStruct((8, t), jnp.int32),
                   jax.ShapeDtypeStruct((t, TOP_K), F32),
                   jax.ShapeDtypeStruct((t // tm, 1, LANES), jnp.int32)],
        compiler_params=_params(("arbitrary",)),
        name="router",
    )(x1, mod, g_ffn.reshape(1, d), wr, br)


def _segment_copies(tile, len_ref, loc_ref, glob_ref, make_copy, wait):
    def per_expert(e, carry):
        j = tile * N_EXPERTS + e
        ln, lo, go = len_ref[j], loc_ref[j], glob_ref[j]
        for b in SEG_BITS:
            above = ln & ~(2 * b - 1)

            @pl.when((ln & b) != 0)
            def _():
                cp = make_copy(pl.multiple_of(lo + above, SEG_ALIGN), pl.multiple_of(go + above, SEG_ALIGN), b)
                cp.wait() if wait else cp.start()
        return carry

    lax.fori_loop(0, N_EXPERTS, per_expert, 0)


def _dispatch_kernel(len_ref, loc_ref, glob_ref, tail_ref, post_ref, x_ref, mod_ref, g_ref, xs_ref,
                     sort_scr, zero_scr, sem):
    i = pl.program_id(0)
    last = pl.num_programs(0) - 1
    slot = i % 2
    tm = x_ref.shape[0]
    r_max = sort_scr.shape[1]

    @pl.when(i == 0)
    def _():
        zero_scr[...] = jnp.zeros_like(zero_scr)
        for wait in (False, True):
            for e in range(N_EXPERTS):
                @pl.when(tail_ref[e] >= 0)
                def _():
                    cp = pltpu.make_async_copy(
                        zero_scr, xs_ref.at[pl.ds(pl.multiple_of(tail_ref[e], SEG_ALIGN), MOE_BLOCK), :],
                        sem.at[0])
                    cp.wait() if wait else cp.start()

    h = _ffn_input(x_ref[...], mod_ref, g_ref).astype(BF16)
    rows = lax.broadcasted_iota(jnp.int32, (r_max, tm), 0)
    sel = rows == post_ref[0:1, :]
    for k in range(1, TOP_K):
        sel = jnp.logical_or(sel, rows == post_ref[k:k + 1, :])
    perm = jnp.where(sel, 1.0, 0.0).astype(BF16)
    sort_scr[slot] = jnp.dot(perm, h, preferred_element_type=F32).astype(BF16)

    def copies_from(buf):
        def make_copy(lo, go, n):
            return pltpu.make_async_copy(sort_scr.at[buf, pl.ds(lo, n), :], xs_ref.at[pl.ds(go, n), :],
                                         sem.at[buf])
        return make_copy

    _segment_copies(i, len_ref, loc_ref, glob_ref, copies_from(slot), wait=False)

    @pl.when(i > 0)
    def _():
        _segment_copies(i - 1, len_ref, loc_ref, glob_ref, copies_from(1 - slot), wait=True)

    @pl.when(i == last)
    def _():
        _segment_copies(i, len_ref, loc_ref, glob_ref, copies_from(slot), wait=True)


def _dispatch(seg_len, seg_loc, seg_glob, tail, pos_t, x1, mod, g_ffn, r_tot, seq):
    t, d = x1.shape
    tm = MOE_TILE
    per_b = seq // tm
    return pl.pallas_call(
        _dispatch_kernel,
        grid_spec=pltpu.PrefetchScalarGridSpec(
            num_scalar_prefetch=4, grid=(t // tm,),
            in_specs=[pl.BlockSpec((8, tm), lambda i, *_: (0, i)),
                      pl.BlockSpec((tm, d), lambda i, *_: (i, 0)),
                      pl.BlockSpec((1, N_MOD, d), lambda i, *_: (i // per_b, 0, 0)),
                      pl.BlockSpec((1, d), lambda i, *_: (0, 0))],
            out_specs=pl.BlockSpec(memory_space=pl.ANY),
            scratch_shapes=[pltpu.VMEM((2, SORT_ROWS, d), BF16), pltpu.VMEM((MOE_BLOCK, d), BF16),
                            pltpu.SemaphoreType.DMA((2,))]),
        out_shape=jax.ShapeDtypeStruct((r_tot, d), BF16),
        compiler_params=_params(("arbitrary",)),
        name="dispatch",
    )(seg_len, seg_loc, seg_glob, tail, pos_t, x1, mod, g_ffn.reshape(1, d))


def _expert_kernel(be_ref, nu_ref, x_ref, wgu_ref, bgu_ref, wd_ref, bd_ref, o_ref):
    @pl.when(pl.program_id(0) < nu_ref[0])
    def _():
        gu = jnp.dot(x_ref[...], wgu_ref[0], preferred_element_type=F32) + bgu_ref[0]
        gate = jnp.minimum(gu[:, :D_FF], SWIGLU_LIMIT)
        up = jnp.clip(gu[:, D_FF:], -SWIGLU_LIMIT, SWIGLU_LIMIT)
        act = (up + 1.0) * gate * jax.nn.sigmoid(SWIGLU_ALPHA * gate)
        y = jnp.dot(act.astype(BF16), wd_ref[0], preferred_element_type=F32) + bd_ref[0]
        o_ref[...] = y.astype(BF16)


def _expert_blocks(block_e, n_used, xs, w_gu, b_gu, w_down, b_down):
    r_tot, d = xs.shape
    n_blocks = r_tot // MOE_BLOCK
    row_map = lambda i, be, nu: (jnp.minimum(i, nu[0] - 1), 0)
    e_map = lambda i, be, nu: (be[i], 0, 0)
    return pl.pallas_call(
        _expert_kernel,
        grid_spec=pltpu.PrefetchScalarGridSpec(
            num_scalar_prefetch=2, grid=(n_blocks,),
            in_specs=[pl.BlockSpec((MOE_BLOCK, d), row_map),
                      pl.BlockSpec((1, d, 2 * D_FF), e_map),
                      pl.BlockSpec((1, 1, 2 * D_FF), e_map),
                      pl.BlockSpec((1, D_FF, d), e_map),
                      pl.BlockSpec((1, 1, d), e_map)],
            out_specs=pl.BlockSpec((MOE_BLOCK, d), row_map)),
        out_shape=jax.ShapeDtypeStruct((r_tot, d), BF16),
        compiler_params=_params(("arbitrary",)),
        name="expert_blocks",
    )(block_e, n_used, xs, w_gu, b_gu.reshape(N_EXPERTS, 1, 2 * D_FF), w_down,
      b_down.reshape(N_EXPERTS, 1, d))


def _combine_kernel(len_ref, loc_ref, glob_ref, pos_ref, w_ref, x_ref, mod_ref, g_ref, y_ref, o_ref,
                    sort_scr, sem, *, final_norm):
    i = pl.program_id(0)
    last = pl.num_programs(0) - 1
    slot = i % 2
    tm = x_ref.shape[0]
    r_max = sort_scr.shape[1]

    def copies_into(buf):
        def make_copy(lo, go, n):
            return pltpu.make_async_copy(y_ref.at[pl.ds(go, n), :], sort_scr.at[buf, pl.ds(lo, n), :],
                                         sem.at[buf])
        return make_copy

    def fetch(tile, buf):
        sort_scr[buf] = jnp.zeros(sort_scr.shape[1:], sort_scr.dtype)
        _segment_copies(tile, len_ref, loc_ref, glob_ref, copies_into(buf), wait=False)

    @pl.when(i == 0)
    def _():
        fetch(i, slot)

    @pl.when(i < last)
    def _():
        fetch(i + 1, 1 - slot)

    _segment_copies(i, len_ref, loc_ref, glob_ref, copies_into(slot), wait=True)

    cols = lax.broadcasted_iota(jnp.int32, (tm, r_max), 1)
    w = w_ref[...]
    pos = pos_ref[...]
    gather_w = jnp.where(cols == pos[:, 0:1], w[:, 0:1], 0.0)
    for k in range(1, TOP_K):
        gather_w = gather_w + jnp.where(cols == pos[:, k:k + 1], w[:, k:k + 1], 0.0)
    moe = jnp.dot(gather_w.astype(BF16), sort_scr[slot], preferred_element_type=F32)
    x2 = x_ref[...] + mod_ref[0, 5:6, :] * moe
    o_ref[...] = _rms(x2, g_ref[...]) if final_norm else x2


def _combine(seg_len, seg_loc, seg_glob, pos, top_w, x1, mod, g_final, y, seq, final_norm):
    t, d = x1.shape
    tm = MOE_TILE
    per_b = seq // tm
    slot_spec = pl.BlockSpec((tm, TOP_K), lambda i, *_: (i, 0))
    return pl.pallas_call(
        functools.partial(_combine_kernel, final_norm=final_norm),
        grid_spec=pltpu.PrefetchScalarGridSpec(
            num_scalar_prefetch=3, grid=(t // tm,),
            in_specs=[slot_spec, slot_spec,
                      pl.BlockSpec((tm, d), lambda i, *_: (i, 0)),
                      pl.BlockSpec((1, N_MOD, d), lambda i, *_: (i // per_b, 0, 0)),
                      pl.BlockSpec((1, d), lambda i, *_: (0, 0)),
                      pl.BlockSpec(memory_space=pl.ANY)],
            out_specs=pl.BlockSpec((tm, d), lambda i, *_: (i, 0)),
            scratch_shapes=[pltpu.VMEM((2, SORT_ROWS, d), BF16), pltpu.SemaphoreType.DMA((2,))]),
        out_shape=jax.ShapeDtypeStruct((t, d), F32),
        compiler_params=_params(("arbitrary",)),
        name="combine",
    )(seg_len, seg_loc, seg_glob, pos, top_w, x1, mod, g_final.reshape(1, d), y)


def _regroup_w_in(w_in):
    o_sb, o_sc, o_sx = 0, D_MODEL, 2 * D_MODEL
    o_z = 3 * D_MODEL
    o_xbc = o_z + D_INNER
    o_dt = o_xbc + D_XBC
    o_ga = o_dt + SSM_HEADS
    o_gb = o_ga + D_MODEL
    main = jnp.concatenate([w_in[:, o_z:o_xbc], w_in[:, o_xbc:o_dt], w_in[:, o_sb:o_sc], w_in[:, o_sc:o_sx],
                            w_in[:, o_sx:o_z], w_in[:, o_ga:o_gb], w_in[:, o_gb:o_gb + D_MODEL]], axis=1)
    w_dt = jnp.pad(w_in[:, o_dt:o_ga], ((0, 0), (0, LANES - SSM_HEADS)))
    return main.astype(BF16), w_dt.astype(BF16)


def _segment_tables(counts, n_blocks):
    seg_len = (counts + SEG_ALIGN - 1) // SEG_ALIGN * SEG_ALIGN
    seg_loc = jnp.cumsum(seg_len, axis=1) - seg_len
    used = jnp.sum(seg_len, axis=0)
    padded = (used + MOE_BLOCK - 1) // MOE_BLOCK * MOE_BLOCK
    pend = jnp.cumsum(padded)
    pstart = pend - padded
    seg_glob = pstart[None, :] + jnp.cumsum(seg_len, axis=0) - seg_len
    tail = jnp.where(padded > 0, pend - MOE_BLOCK, -1)
    first_row = jnp.arange(n_blocks, dtype=jnp.int32) * MOE_BLOCK
    block_e = jnp.sum((pend[None, :] <= first_row[:, None]).astype(jnp.int32), axis=1)
    block_e = jnp.minimum(block_e, N_EXPERTS - 1)
    n_used = pend[-1:] // MOE_BLOCK
    i32 = lambda a: a.astype(jnp.int32)
    return (i32(seg_len).reshape(-1), i32(seg_loc).reshape(-1), i32(seg_glob).reshape(-1), i32(tail),
            i32(block_e), i32(n_used))


def kernel(x, c, w_ada, b_ada, g_mix, w_in, w_sconv, w_sconv_out, w_ssm_conv, b_ssm_conv, dt_bias, a_log,
           d_skip, g_ssm_norm, w_ssm_out, w_o, g_ffn, w_router, b_router, w_gu, b_gu, w_down, b_down,
           g_final):
    bsz, seq, d = x.shape
    depth = w_ada.shape[0]
    t = bsz * seq
    assert seq % MOE_TILE == 0 and seq % SSM_CHUNK == 0
    n_tiles = t // MOE_TILE
    worst_rows = t * TOP_K + n_tiles * N_EXPERTS * (SEG_ALIGN - 1) + N_EXPERTS * (MOE_BLOCK - 1)
    n_blocks = -(-worst_rows // MOE_BLOCK)
    r_tot = n_blocks * MOE_BLOCK
    xf = x.reshape(t, d)
    for l in range(depth):
        mod = _modulation(c, w_ada[l], b_ada[l]).reshape(bsz, N_MOD, d)
        w_main, w_dt = _regroup_w_in(w_in[l])
        proj, dt_raw = _in_projection(xf, mod, g_mix[l], w_main, w_dt, seq)
        x1 = _token_mixer(proj, dt_raw, xf, mod, w_sconv[l], w_ssm_conv[l], b_ssm_conv[l], dt_bias[l],
                          a_log[l], d_skip[l], g_ssm_norm[l], w_sconv_out[l].astype(BF16),
                          w_ssm_out[l].astype(BF16), w_o[l].astype(BF16), bsz, seq)
        pos, pos_t, top_w, counts = _router(x1, mod, g_ffn[l], w_router[l], b_router[l], seq)
        seg_len, seg_loc, seg_glob, tail, block_e, n_used = _segment_tables(
            counts[:, 0, :N_EXPERTS], n_blocks)
        xs = _dispatch(seg_len, seg_loc, seg_glob, tail, pos_t, x1, mod, g_ffn[l], r_tot, seq)
        ys = _expert_blocks(block_e, n_used, xs, w_gu[l].astype(BF16), b_gu[l], w_down[l].astype(BF16),
                            b_down[l])
        xf = _combine(seg_len, seg_loc, seg_glob, pos, top_w, x1, mod, g_final, ys, seq,
                      final_norm=(l == depth - 1))
    return xf.reshape(bsz, seq, d)
```

```python
import functools

import jax
import jax.numpy as jnp
from jax import lax
from jax.experimental import pallas as pl
from jax.experimental.pallas import tpu as pltpu

F32 = jnp.float32
BF16 = jnp.bfloat16

D_MODEL = 1024
CONV_K = 3
D_INNER = 2048
SSM_HEAD_DIM = 64
SSM_HEADS = 32
SSM_GROUPS = 4
SSM_STATE = 128
SSM_CONV_K = 4
SSM_CHUNK = 256
D_XBC = D_INNER + 2 * SSM_GROUPS * SSM_STATE
N_EXPERTS = 32
TOP_K = 4
D_FF = D_MODEL
SWIGLU_LIMIT = 7.0
SWIGLU_ALPHA = 1.702
EPS = 1e-6
N_MOD = 6

HEADS_PER_GROUP = SSM_HEADS // SSM_GROUPS
GROUP_WIDTH = D_INNER // SSM_GROUPS
LANES = 128
HALO = 8

MOE_BLOCK = 512
MOE_TILE = 512
SEG_ALIGN = 16
MXU_DIM = 256
SORT_ROWS = -(-(MOE_TILE * TOP_K + N_EXPERTS * (SEG_ALIGN - 1)) // MXU_DIM) * MXU_DIM

P_Z = 0
P_XBC = P_Z + D_INNER
P_SB = P_XBC + D_XBC
P_SC = P_SB + D_MODEL
P_SX = P_SC + D_MODEL
P_GA = P_SX + D_MODEL
P_GB = P_GA + D_MODEL
P_TOTAL = P_GB + D_MODEL

VMEM_LIMIT = 56 * 1024 * 1024


def _params(sem):
    return pltpu.CompilerParams(dimension_semantics=sem, vmem_limit_bytes=VMEM_LIMIT)


def _rms(x, g):
    return x * lax.rsqrt(jnp.mean(x * x, axis=-1, keepdims=True) + EPS) * g


def _silu(x):
    return x * jax.nn.sigmoid(x)


def _mod_kernel(c_ref, w_ref, b_ref, o_ref):
    c = c_ref[...]
    cond = _silu(c)
    o_ref[...] = jnp.dot(cond, w_ref[...], preferred_element_type=F32,
                         precision=lax.Precision.HIGHEST) + b_ref[...]


def _modulation(c, w_ada, b_ada):
    bsz, d = c.shape
    n = w_ada.shape[1]
    bn = 1024
    return pl.pallas_call(
        _mod_kernel,
        grid=(n // bn,),
        in_specs=[pl.BlockSpec((bsz, d), lambda j: (0, 0)),
                  pl.BlockSpec((d, bn), lambda j: (0, j)),
                  pl.BlockSpec((1, bn), lambda j: (0, j))],
        out_specs=pl.BlockSpec((bsz, bn), lambda j: (0, j)),
        out_shape=jax.ShapeDtypeStruct((bsz, n), F32),
        compiler_params=_params(("arbitrary",)),
        name="adaln_mod",
    )(c, w_ada, b_ada.reshape(1, n))


def _inproj_kernel(x_ref, mod_ref, g_ref, w_ref, wdt_ref, o_ref, dt_ref):
    sh = mod_ref[0, 0:1, :]
    sc = mod_ref[0, 1:2, :]
    h = (_rms(x_ref[...], g_ref[...]) * (1.0 + sc) + sh).astype(BF16)
    dt_ref[...] = jnp.dot(h, wdt_ref[...], preferred_element_type=F32)
    o_ref[...] = jnp.dot(h, w_ref[...], preferred_element_type=F32).astype(BF16)


def _in_projection(x2d, mod, g_mix, w_main, w_dt, seq):
    t, d = x2d.shape
    n = w_main.shape[1]
    tm = min(512, seq)
    per_b = seq // tm
    return pl.pallas_call(
        _inproj_kernel,
        grid=(t // tm,),
        in_specs=[pl.BlockSpec((tm, d), lambda i: (i, 0)),
                  pl.BlockSpec((1, N_MOD, d), lambda i: (i // per_b, 0, 0)),
                  pl.BlockSpec((1, d), lambda i: (0, 0)),
                  pl.BlockSpec((d, n), lambda i: (0, 0)),
                  pl.BlockSpec((d, LANES), lambda i: (0, 0))],
        out_specs=[pl.BlockSpec((tm, n), lambda i: (i, 0)),
                   pl.BlockSpec((tm, LANES), lambda i: (i, 0))],
        out_shape=[jax.ShapeDtypeStruct((t, n), BF16),
                   jax.ShapeDtypeStruct((t, LANES), F32)],
        compiler_params=_params(("arbitrary",)),
        name="in_proj",
    )(x2d, mod, g_mix.reshape(1, d), w_main, w_dt)


def _split3(v):
    hi = v.astype(BF16)
    r1 = v - hi.astype(F32)
    mid = r1.astype(BF16)
    lo = (r1 - mid.astype(F32)).astype(BF16)
    return hi, mid, lo


def _mixer_kernel(proj_ref, dt_ref, x_ref, mod_ref, wsc_ref, wcv_ref, bcv_ref, dtb_ref, alog_ref,
                  dskip_ref, gn_ref, e_ref, wa_ref, wb_ref, wo_ref, o_ref,
                  s_scr, hx_scr, hu_scr, extx, extu):
    L = SSM_CHUNK

    @pl.when(pl.program_id(1) == 0)
    def _():
        s_scr[...] = jnp.zeros_like(s_scr)
        hx_scr[...] = jnp.zeros_like(hx_scr)
        hu_scr[...] = jnp.zeros_like(hu_scr)

    def col(lo, width):
        return proj_ref[:, lo:lo + width].astype(F32)

    u = col(P_SB, D_MODEL) * col(P_SX, D_MODEL)
    extu[0:HALO, :] = hu_scr[...]
    extu[HALO:HALO + L, :] = u
    conv_a = wsc_ref[CONV_K - 1:CONV_K, :] * u
    for k in range(CONV_K - 1):
        back = CONV_K - 1 - k
        conv_a = conv_a + wsc_ref[k:k + 1, :] * extu[HALO - back:HALO - back + L, :]
    hu_scr[...] = extu[L:L + HALO, :]
    ya_in = (col(P_SC, D_MODEL) * conv_a).astype(BF16)
    y_a = jnp.dot(ya_in, wa_ref[...], preferred_element_type=F32)

    xbc = col(P_XBC, D_XBC)
    extx[0:HALO, :] = hx_scr[...]
    extx[HALO:HALO + L, :] = xbc
    cv = wcv_ref[SSM_CONV_K - 1:SSM_CONV_K, :] * xbc + bcv_ref[...]
    for k in range(SSM_CONV_K - 1):
        back = SSM_CONV_K - 1 - k
        cv = cv + wcv_ref[k:k + 1, :] * extx[HALO - back:HALO - back + L, :]
    hx_scr[...] = extx[L:L + HALO, :]
    act = _silu(cv)
    xs = act[:, :D_INNER]
    bs = act[:, D_INNER:D_INNER + SSM_GROUPS * SSM_STATE]
    cs = act[:, D_INNER + SSM_GROUPS * SSM_STATE:]

    dt = jax.nn.softplus(dt_ref[...] + dtb_ref[...])
    a = -jnp.exp(alog_ref[...])
    da = dt * a
    row = lax.broadcasted_iota(jnp.int32, (L, L), 0)
    cidx = lax.broadcasted_iota(jnp.int32, (L, L), 1)
    causal = cidx <= row
    tri = jnp.where(causal, 1.0, 0.0).astype(BF16)
    a_cum = sum(jnp.dot(tri, p, preferred_element_type=F32) for p in _split3(da))
    a_cum_t = a_cum.T
    ea = jnp.exp(a_cum)
    de = jnp.exp(a_cum[L - 1:L, :] - a_cum)
    e_mat = e_ref[...]
    dt_e = jnp.dot(dt.astype(BF16), e_mat, preferred_element_type=F32)
    w_e = jnp.dot((dt * de).astype(BF16), e_mat, preferred_element_type=F32)
    ea_e = jnp.dot(ea.astype(BF16), e_mat, preferred_element_type=F32)
    xdt = (xs * dt_e).astype(BF16)
    xw = (xs * w_e).astype(BF16)
    bs_t = bs.T
    lane = lax.broadcasted_iota(jnp.int32, (L, 2 * SSM_HEAD_DIM), 1)
    lo_half = lane < SSM_HEAD_DIM

    y_groups = []
    for g in range(SSM_GROUPS):
        cg = cs[:, g * SSM_STATE:(g + 1) * SSM_STATE].astype(BF16)
        bg = bs[:, g * SSM_STATE:(g + 1) * SSM_STATE].astype(BF16)
        bg_t = bs_t[g * SSM_STATE:(g + 1) * SSM_STATE, :].astype(BF16)
        cb = lax.dot_general(cg, bg, (((1,), (1,)), ((), ())), preferred_element_type=F32)
        s_old = s_scr[g]
        gsl = slice(g * GROUP_WIDTH, (g + 1) * GROUP_WIDTH)
        y_off = jnp.dot(cg, s_old.astype(BF16), preferred_element_type=F32) * ea_e[:, gsl]
        pieces = []
        for p in range(HEADS_PER_GROUP // 2):
            acc = None
            xpair = xdt[:, g * GROUP_WIDTH + p * 2 * SSM_HEAD_DIM:g * GROUP_WIDTH + (p + 1) * 2 * SSM_HEAD_DIM]
            for half in range(2):
                h = g * HEADS_PER_GROUP + 2 * p + half
                seg = a_cum[:, h:h + 1] - a_cum_t[h:h + 1, :]
                decay = jnp.exp(jnp.where(causal, seg, -jnp.inf))
                scores = (cb * decay).astype(BF16)
                keep = lo_half if half == 0 else jnp.logical_not(lo_half)
                part = jnp.dot(scores, jnp.where(keep, xpair, jnp.zeros_like(xpair)),
                               preferred_element_type=F32)
                acc = part if acc is None else acc + part
            pieces.append(acc)
        y_groups.append(jnp.concatenate(pieces, axis=1) + y_off)
        s_new = s_old * ea_e[L - 1:L, gsl] + jnp.dot(bg_t, xw[:, gsl], preferred_element_type=F32)
        s_scr[g] = s_new
    y = jnp.concatenate(y_groups, axis=1) + dskip_ref[...] * xs

    yf = y * _silu(col(P_Z, D_INNER))
    normed = []
    for g in range(SSM_GROUPS):
        blk = yf[:, g * GROUP_WIDTH:(g + 1) * GROUP_WIDTH]
        normed.append(blk * lax.rsqrt(jnp.mean(blk * blk, axis=-1, keepdims=True) + EPS))
    yn = (jnp.concatenate(normed, axis=1) * gn_ref[...]).astype(BF16)
    y_b = jnp.dot(yn, wb_ref[...], preferred_element_type=F32)
    m = jax.nn.sigmoid(col(P_GA, D_MODEL)) * y_a + jax.nn.sigmoid(col(P_GB, D_MODEL)) * y_b
    mix = jnp.dot(m.astype(BF16), wo_ref[...], preferred_element_type=F32)
    o_ref[...] = x_ref[...] + mod_ref[0, 2:3, :] * mix


def _token_mixer(proj, dt_raw, x2d, mod, w_sconv, w_ssm_conv, b_ssm_conv, dt_bias, a_log, d_skip,
                 g_ssm_norm, w_a, w_b, w_o, bsz, seq):
    t, d = x2d.shape
    L = SSM_CHUNK
    nc = seq // L
    pad_h = LANES - SSM_HEADS
    dtb = jnp.pad(dt_bias, (0, pad_h)).reshape(1, LANES)
    alog = jnp.pad(a_log, (0, pad_h)).reshape(1, LANES)
    dskip_e = jnp.repeat(d_skip, SSM_HEAD_DIM).reshape(1, D_INNER)
    expand = (jnp.arange(D_INNER)[None, :] // SSM_HEAD_DIM == jnp.arange(LANES)[:, None]).astype(BF16)
    full = lambda shape: pl.BlockSpec(shape, lambda b, c: (0,) * len(shape))
    return pl.pallas_call(
        _mixer_kernel,
        grid=(bsz, nc),
        in_specs=[pl.BlockSpec((L, P_TOTAL), lambda b, c: (b * nc + c, 0)),
                  pl.BlockSpec((L, LANES), lambda b, c: (b * nc + c, 0)),
                  pl.BlockSpec((L, d), lambda b, c: (b * nc + c, 0)),
                  pl.BlockSpec((1, N_MOD, d), lambda b, c: (b, 0, 0)),
                  full((CONV_K, D_MODEL)), full((SSM_CONV_K, D_XBC)), full((1, D_XBC)),
                  full((1, LANES)), full((1, LANES)), full((1, D_INNER)), full((1, D_INNER)),
                  full((LANES, D_INNER)),
                  full((D_MODEL, D_MODEL)), full((D_INNER, D_MODEL)), full((D_MODEL, D_MODEL))],
        out_specs=pl.BlockSpec((L, d), lambda b, c: (b * nc + c, 0)),
        out_shape=jax.ShapeDtypeStruct((t, d), F32),
        scratch_shapes=[pltpu.VMEM((SSM_GROUPS, SSM_STATE, GROUP_WIDTH), F32),
                        pltpu.VMEM((HALO, D_XBC), F32),
                        pltpu.VMEM((HALO, D_MODEL), F32),
                        pltpu.VMEM((L + HALO, D_XBC), F32),
                        pltpu.VMEM((L + HALO, D_MODEL), F32)],
        compiler_params=_params(("arbitrary", "arbitrary")),
        name="token_mixer",
    )(proj, dt_raw, x2d, mod, w_sconv, w_ssm_conv, b_ssm_conv.reshape(1, D_XBC), dtb, alog, dskip_e,
      g_ssm_norm.reshape(1, D_INNER), expand, w_a, w_b, w_o)


def _ffn_input(x, mod_ref, g_ref):
    return _rms(x, g_ref[...]) * (1.0 + mod_ref[0, 4:5, :]) + mod_ref[0, 3:4, :]


def _router_kernel(x_ref, mod_ref, g_ref, wr_ref, br_ref, pos_ref, post_ref, w_ref, cnt_ref):
    tm = x_ref.shape[0]
    h = _ffn_input(x_ref[...], mod_ref, g_ref)
    logits = jnp.dot(h, wr_ref[...], preferred_element_type=F32,
                     precision=lax.Precision.HIGHEST) + br_ref[...]
    lane = lax.broadcasted_iota(jnp.int32, (tm, LANES), 1)
    work = jnp.where(lane < N_EXPERTS, logits, -jnp.inf)
    vals, hots = [], []
    for _ in range(TOP_K):
        v = jnp.max(work, axis=-1, keepdims=True)
        i = jnp.min(jnp.where(work == v, lane, LANES), axis=-1, keepdims=True)
        hot = lane == i
        vals.append(v)
        hots.append(hot)
        work = jnp.where(hot, -jnp.inf, work)
    exps = [jnp.exp(v - vals[0]) for v in vals]
    denom = exps[0] + exps[1] + exps[2] + exps[3]

    onehot = jnp.zeros((tm, LANES), F32)
    for hot in hots:
        onehot = onehot + jnp.where(hot, 1.0, 0.0)
    r = lax.broadcasted_iota(jnp.int32, (tm, tm), 0)
    c = lax.broadcasted_iota(jnp.int32, (tm, tm), 1)
    strict = jnp.where(c < r, 1.0, 0.0).astype(BF16)
    before = jnp.dot(strict, onehot.astype(BF16), preferred_element_type=F32)
    cnt = jnp.sum(onehot, axis=0, keepdims=True).astype(jnp.int32)
    seg_units = (cnt + (SEG_ALIGN - 1)) // SEG_ALIGN
    er = lax.broadcasted_iota(jnp.int32, (LANES, LANES), 0)
    ec = lax.broadcasted_iota(jnp.int32, (LANES, LANES), 1)
    upper = jnp.where(er < ec, 1.0, 0.0).astype(BF16)
    units8 = jnp.broadcast_to(seg_units.astype(F32), (8, LANES)).astype(BF16)
    seg_off = jnp.dot(units8, upper, preferred_element_type=F32)[0:1, :] * float(SEG_ALIGN)
    where_to = before + seg_off
    packed = jnp.zeros((tm, LANES), F32)
    for k in range(TOP_K):
        pos = jnp.sum(jnp.where(hots[k], where_to, 0.0), axis=-1, keepdims=True)
        pos_ref[:, k:k + 1] = pos.astype(jnp.int32)
        w_ref[:, k:k + 1] = exps[k] / denom
        packed = jnp.where(lane == k, pos, packed)
    post_ref[...] = packed.T[0:8, :].astype(jnp.int32)
    cnt_ref[0] = cnt


def _router(x1, mod, g_ffn, w_router, b_router, seq):
    t, d = x1.shape
    tm = MOE_TILE
    per_b = seq // tm
    wr = jnp.pad(w_router, ((0, 0), (0, LANES - N_EXPERTS)))
    br = jnp.pad(b_router, (0, LANES - N_EXPERTS)).reshape(1, LANES)
    slot_spec = pl.BlockSpec((tm, TOP_K), lambda i: (i, 0))
    return pl.pallas_call(
        _router_kernel,
        grid=(t // tm,),
        in_specs=[pl.BlockSpec((tm, d), lambda i: (i, 0)),
                  pl.BlockSpec((1, N_MOD, d), lambda i: (i // per_b, 0, 0)),
                  pl.BlockSpec((1, d), lambda i: (0, 0)),
                  pl.BlockSpec((d, LANES), lambda i: (0, 0)),
                  pl.BlockSpec((1, LANES), lambda i: (0, 0))],
        out_specs=[slot_spec, pl.BlockSpec((8, tm), lambda i: (0, i)), slot_spec,
                   pl.BlockSpec((1, 1, LANES), lambda i: (i, 0, 0))],
        out_shape=[jax.ShapeDtypeStruct((t, TOP_K), jnp.int32),
                   jax.ShapeDtypeStruct((8, t), jnp.int32),
                   jax.ShapeDtypeStruct((t, TOP_K), F32),
                   jax.ShapeDtypeStruct((t // tm, 1, LANES), jnp.int32)],
        compiler_params=_params(("arbitrary",)),
        name="router",
    )(x1, mod, g_ffn.reshape(1, d), wr, br)


def _segment_copies(tile, len_ref, loc_ref, glob_ref, make_copy, wait):
    if wait:
        end = tile * N_EXPERTS + N_EXPERTS - 1
        rows = loc_ref[end] + len_ref[end]
        make_copy(0, 0, pl.multiple_of(rows, SEG_ALIGN)).wait()
        return

    def per_expert(e, carry):
        j = tile * N_EXPERTS + e
        ln, lo, go = len_ref[j], loc_ref[j], glob_ref[j]

        @pl.when(ln > 0)
        def _():
            make_copy(pl.multiple_of(lo, SEG_ALIGN), pl.multiple_of(go, SEG_ALIGN),
                      pl.multiple_of(ln, SEG_ALIGN)).start()
        return carry

    lax.fori_loop(0, N_EXPERTS, per_expert, 0)


def _dispatch_kernel(len_ref, loc_ref, glob_ref, tail_ref, post_ref, x_ref, mod_ref, g_ref, xs_ref,
                     sort_scr, zero_scr, sem):
    i = pl.program_id(0)
    last = pl.num_programs(0) - 1
    slot = i % 2
    tm = x_ref.shape[0]
    r_max = sort_scr.shape[1]

    @pl.when(i == 0)
    def _():
        zero_scr[...] = jnp.zeros_like(zero_scr)
        for wait in (False, True):
            for e in range(N_EXPERTS):
                @pl.when(tail_ref[e] >= 0)
                def _():
                    cp = pltpu.make_async_copy(
                        zero_scr, xs_ref.at[pl.ds(pl.multiple_of(tail_ref[e], SEG_ALIGN), MOE_BLOCK), :],
                        sem.at[0])
                    cp.wait() if wait else cp.start()

    h = _ffn_input(x_ref[...], mod_ref, g_ref).astype(BF16)
    rows = lax.broadcasted_iota(jnp.int32, (r_max, tm), 0)
    perm = jnp.zeros((r_max, tm), F32)
    for k in range(TOP_K):
        perm = jnp.where(rows == post_ref[k:k + 1, :], 1.0, perm)
    sort_scr[slot] = jnp.dot(perm.astype(BF16), h, preferred_element_type=F32).astype(BF16)

    def copies_from(buf):
        def make_copy(lo, go, n):
            return pltpu.make_async_copy(sort_scr.at[buf, pl.ds(lo, n), :], xs_ref.at[pl.ds(go, n), :],
                                         sem.at[buf])
        return make_copy

    _segment_copies(i, len_ref, loc_ref, glob_ref, copies_from(slot), wait=False)

    @pl.when(i > 0)
    def _():
        _segment_copies(i - 1, len_ref, loc_ref, glob_ref, copies_from(1 - slot), wait=True)

    @pl.when(i == last)
    def _():
        _segment_copies(i, len_ref, loc_ref, glob_ref, copies_from(slot), wait=True)


def _dispatch(seg_len, seg_loc, seg_glob, tail, pos_t, x1, mod, g_ffn, r_tot, seq):
    t, d = x1.shape
    tm = MOE_TILE
    per_b = seq // tm
    return pl.pallas_call(
        _dispatch_kernel,
        grid_spec=pltpu.PrefetchScalarGridSpec(
            num_scalar_prefetch=4, grid=(t // tm,),
            in_specs=[pl.BlockSpec((8, tm), lambda i, *_: (0, i)),
                      pl.BlockSpec((tm, d), lambda i, *_: (i, 0)),
                      pl.BlockSpec((1, N_MOD, d), lambda i, *_: (i // per_b, 0, 0)),
                      pl.BlockSpec((1, d), lambda i, *_: (0, 0))],
            out_specs=pl.BlockSpec(memory_space=pl.ANY),
            scratch_shapes=[pltpu.VMEM((2, SORT_ROWS, d), BF16), pltpu.VMEM((MOE_BLOCK, d), BF16),
                            pltpu.SemaphoreType.DMA((2,))]),
        out_shape=jax.ShapeDtypeStruct((r_tot, d), BF16),
        compiler_params=_params(("arbitrary",)),
        name="dispatch",
    )(seg_len, seg_loc, seg_glob, tail, pos_t, x1, mod, g_ffn.reshape(1, d))


def _expert_kernel(be_ref, nu_ref, x_ref, wgu_ref, bgu_ref, wd_ref, bd_ref, o_ref):
    @pl.when(pl.program_id(0) < nu_ref[0])
    def _():
        gu = jnp.dot(x_ref[...], wgu_ref[0], preferred_element_type=F32) + bgu_ref[0]
        gate = jnp.minimum(gu[:, :D_FF], SWIGLU_LIMIT)
        up = jnp.clip(gu[:, D_FF:], -SWIGLU_LIMIT, SWIGLU_LIMIT)
        act = (up + 1.0) * gate * jax.nn.sigmoid(SWIGLU_ALPHA * gate)
        y = jnp.dot(act.astype(BF16), wd_ref[0], preferred_element_type=F32) + bd_ref[0]
        o_ref[...] = y.astype(BF16)


def _expert_blocks(block_e, n_used, xs, w_gu, b_gu, w_down, b_down):
    r_tot, d = xs.shape
    n_blocks = r_tot // MOE_BLOCK
    row_map = lambda i, be, nu: (jnp.minimum(i, nu[0] - 1), 0)
    e_map = lambda i, be, nu: (be[i], 0, 0)
    return pl.pallas_call(
        _expert_kernel,
        grid_spec=pltpu.PrefetchScalarGridSpec(
            num_scalar_prefetch=2, grid=(n_blocks,),
            in_specs=[pl.BlockSpec((MOE_BLOCK, d), row_map),
                      pl.BlockSpec((1, d, 2 * D_FF), e_map),
                      pl.BlockSpec((1, 1, 2 * D_FF), e_map),
                      pl.BlockSpec((1, D_FF, d), e_map),
                      pl.BlockSpec((1, 1, d), e_map)],
            out_specs=pl.BlockSpec((MOE_BLOCK, d), row_map)),
        out_shape=jax.ShapeDtypeStruct((r_tot, d), BF16),
        compiler_params=_params(("arbitrary",)),
        name="expert_blocks",
    )(block_e, n_used, xs, w_gu, b_gu.reshape(N_EXPERTS, 1, 2 * D_FF), w_down,
      b_down.reshape(N_EXPERTS, 1, d))


def _combine_kernel(len_ref, loc_ref, glob_ref, pos_ref, w_ref, x_ref, mod_ref, g_ref, y_ref, o_ref,
                    sort_scr, sem, *, final_norm):
    i = pl.program_id(0)
    last = pl.num_programs(0) - 1
    slot = i % 2
    tm = x_ref.shape[0]
    r_max = sort_scr.shape[1]

    def copies_into(buf):
        def make_copy(lo, go, n):
            return pltpu.make_async_copy(y_ref.at[pl.ds(go, n), :], sort_scr.at[buf, pl.ds(lo, n), :],
                                         sem.at[buf])
        return make_copy

    def fetch(tile, buf):
        sort_scr[buf] = jnp.zeros(sort_scr.shape[1:], sort_scr.dtype)
        _segment_copies(tile, len_ref, loc_ref, glob_ref, copies_into(buf), wait=False)

    @pl.when(i == 0)
    def _():
        fetch(i, slot)

    @pl.when(i < last)
    def _():
        fetch(i + 1, 1 - slot)

    _segment_copies(i, len_ref, loc_ref, glob_ref, copies_into(slot), wait=True)

    cols = lax.broadcasted_iota(jnp.int32, (tm, r_max), 1)
    w = w_ref[...]
    pos = pos_ref[...]
    gather_w = jnp.zeros((tm, r_max), F32)
    for k in range(TOP_K):
        gather_w = jnp.where(cols == pos[:, k:k + 1], w[:, k:k + 1], gather_w)
    moe = jnp.dot(gather_w.astype(BF16), sort_scr[slot], preferred_element_type=F32)
    x2 = x_ref[...] + mod_ref[0, 5:6, :] * moe
    o_ref[...] = _rms(x2, g_ref[...]) if final_norm else x2


def _combine(seg_len, seg_loc, seg_glob, pos, top_w, x1, mod, g_final, y, seq, final_norm):
    t, d = x1.shape
    tm = MOE_TILE
    per_b = seq // tm
    slot_spec = pl.BlockSpec((tm, TOP_K), lambda i, *_: (i, 0))
    return pl.pallas_call(
        functools.partial(_combine_kernel, final_norm=final_norm),
        grid_spec=pltpu.PrefetchScalarGridSpec(
            num_scalar_prefetch=3, grid=(t // tm,),
            in_specs=[slot_spec, slot_spec,
                      pl.BlockSpec((tm, d), lambda i, *_: (i, 0)),
                      pl.BlockSpec((1, N_MOD, d), lambda i, *_: (i // per_b, 0, 0)),
                      pl.BlockSpec((1, d), lambda i, *_: (0, 0)),
                      pl.BlockSpec(memory_space=pl.ANY)],
            out_specs=pl.BlockSpec((tm, d), lambda i, *_: (i, 0)),
            scratch_shapes=[pltpu.VMEM((2, SORT_ROWS, d), BF16), pltpu.SemaphoreType.DMA((2,))]),
        out_shape=jax.ShapeDtypeStruct((t, d), F32),
        compiler_params=_params(("arbitrary",)),
        name="combine",
    )(seg_len, seg_loc, seg_glob, pos, top_w, x1, mod, g_final.reshape(1, d), y)


def _regroup_w_in(w_in):
    o_sb, o_sc, o_sx = 0, D_MODEL, 2 * D_MODEL
    o_z = 3 * D_MODEL
    o_xbc = o_z + D_INNER
    o_dt = o_xbc + D_XBC
    o_ga = o_dt + SSM_HEADS
    o_gb = o_ga + D_MODEL
    main = jnp.concatenate([w_in[:, o_z:o_xbc], w_in[:, o_xbc:o_dt], w_in[:, o_sb:o_sc], w_in[:, o_sc:o_sx],
                            w_in[:, o_sx:o_z], w_in[:, o_ga:o_gb], w_in[:, o_gb:o_gb + D_MODEL]], axis=1)
    w_dt = jnp.pad(w_in[:, o_dt:o_ga], ((0, 0), (0, LANES - SSM_HEADS)))
    return main.astype(BF16), w_dt.astype(BF16)


def _segment_tables(counts, n_blocks):
    seg_len = (counts + SEG_ALIGN - 1) // SEG_ALIGN * SEG_ALIGN
    seg_loc = jnp.cumsum(seg_len, axis=1) - seg_len
    used = jnp.sum(seg_len, axis=0)
    padded = (used + MOE_BLOCK - 1) // MOE_BLOCK * MOE_BLOCK
    pend = jnp.cumsum(padded)
    pstart = pend - padded
    seg_glob = pstart[None, :] + jnp.cumsum(seg_len, axis=0) - seg_len
    tail = jnp.where(padded > 0, pend - MOE_BLOCK, -1)
    first_row = jnp.arange(n_blocks, dtype=jnp.int32) * MOE_BLOCK
    block_e = jnp.sum((pend[None, :] <= first_row[:, None]).astype(jnp.int32), axis=1)
    block_e = jnp.minimum(block_e, N_EXPERTS - 1)
    n_used = pend[-1:] // MOE_BLOCK
    i32 = lambda a: a.astype(jnp.int32)
    return (i32(seg_len).reshape(-1), i32(seg_loc).reshape(-1), i32(seg_glob).reshape(-1), i32(tail),
            i32(block_e), i32(n_used))


def kernel(x, c, w_ada, b_ada, g_mix, w_in, w_sconv, w_sconv_out, w_ssm_conv, b_ssm_conv, dt_bias, a_log,
           d_skip, g_ssm_norm, w_ssm_out, w_o, g_ffn, w_router, b_router, w_gu, b_gu, w_down, b_down,
           g_final):
    bsz, seq, d = x.shape
    depth = w_ada.shape[0]
    t = bsz * seq
    assert seq % MOE_TILE == 0 and seq % SSM_CHUNK == 0
    n_tiles = t // MOE_TILE
    worst_rows = t * TOP_K + n_tiles * N_EXPERTS * (SEG_ALIGN - 1) + N_EXPERTS * (MOE_BLOCK - 1)
    n_blocks = -(-worst_rows // MOE_BLOCK)
    r_tot = n_blocks * MOE_BLOCK
    xf = x.reshape(t, d)
    for l in range(depth):
        mod = _modulation(c, w_ada[l], b_ada[l]).reshape(bsz, N_MOD, d)
        w_main, w_dt = _regroup_w_in(w_in[l])
        proj, dt_raw = _in_projection(xf, mod, g_mix[l], w_main, w_dt, seq)
        x1 = _token_mixer(proj, dt_raw, xf, mod, w_sconv[l], w_ssm_conv[l], b_ssm_conv[l], dt_bias[l],
                          a_log[l], d_skip[l], g_ssm_norm[l], w_sconv_out[l].astype(BF16),
                          w_ssm_out[l].astype(BF16), w_o[l].astype(BF16), bsz, seq)
        pos, pos_t, top_w, counts = _router(x1, mod, g_ffn[l], w_router[l], b_router[l], seq)
        seg_len, seg_loc, seg_glob, tail, block_e, n_used = _segment_tables(
            counts[:, 0, :N_EXPERTS], n_blocks)
        xs = _dispatch(seg_len, seg_loc, seg_glob, tail, pos_t, x1, mod, g_ffn[l], r_tot, seq)
        ys = _expert_blocks(block_e, n_used, xs, w_gu[l].astype(BF16), b_gu[l], w_down[l].astype(BF16),
                            b_down[l])
        xf = _combine(seg_len, seg_loc, seg_glob, pos, top_w, x1, mod, g_final, ys, seq,
                      final_norm=(l == depth - 1))
    return xf.reshape(bsz, seq, d)
```

```python
import functools

import jax
import jax.numpy as jnp
from jax import lax
from jax.experimental import pallas as pl
from jax.experimental.pallas import tpu as pltpu

F32 = jnp.float32
BF16 = jnp.bfloat16

D_MODEL = 1024
CONV_K = 3
D_INNER = 2048
SSM_HEAD_DIM = 64
SSM_HEADS = 32
SSM_GROUPS = 4
SSM_STATE = 128
SSM_CONV_K = 4
SSM_CHUNK = 256
D_XBC = D_INNER + 2 * SSM_GROUPS * SSM_STATE
N_EXPERTS = 32
TOP_K = 4
D_FF = D_MODEL
SWIGLU_LIMIT = 7.0
SWIGLU_ALPHA = 1.702
EPS = 1e-6
N_MOD = 6

HEADS_PER_GROUP = SSM_HEADS // SSM_GROUPS
GROUP_WIDTH = D_INNER // SSM_GROUPS
LANES = 128
HALO = 8

MOE_BLOCK = 512
MOE_TILE = 512
SEG_ALIGN = 16
MXU_DIM = 256
SORT_ROWS = -(-(MOE_TILE * TOP_K + N_EXPERTS * (SEG_ALIGN - 1)) // MXU_DIM) * MXU_DIM

W_Z = 0
W_XBC = W_Z + D_INNER
W_SB = W_XBC + D_XBC
W_SC = W_SB + D_MODEL
W_SX = W_SC + D_MODEL
W_GA = W_SX + D_MODEL
W_GB = W_GA + D_MODEL
W_TOTAL = W_GB + D_MODEL

P_SZ = 0
P_ACT = P_SZ + D_INNER
P_YA = P_ACT + D_XBC
P_GA = P_YA + D_MODEL
P_GB = P_GA + D_MODEL
P_TOTAL = P_GB + D_MODEL
STRIP = 1024
ROWS = 128
N_ACC = 4

VMEM_LIMIT = 56 * 1024 * 1024


def _params(sem):
    return pltpu.CompilerParams(dimension_semantics=sem, vmem_limit_bytes=VMEM_LIMIT)


def _rms(x, g):
    return x * lax.rsqrt(jnp.mean(x * x, axis=-1, keepdims=True) + EPS) * g


def _silu(x):
    return x * jax.nn.sigmoid(x)


def _mod_kernel(c_ref, w_ref, b_ref, o_ref):
    c = c_ref[...]
    cond = _silu(c)
    o_ref[...] = jnp.dot(cond, w_ref[...], preferred_element_type=F32,
                         precision=lax.Precision.HIGHEST) + b_ref[...]


def _modulation(c, w_ada, b_ada):
    bsz, d = c.shape
    n = w_ada.shape[1]
    bn = 1024
    return pl.pallas_call(
        _mod_kernel,
        grid=(n // bn,),
        in_specs=[pl.BlockSpec((bsz, d), lambda j: (0, 0)),
                  pl.BlockSpec((d, bn), lambda j: (0, j)),
                  pl.BlockSpec((1, bn), lambda j: (0, j))],
        out_specs=pl.BlockSpec((bsz, bn), lambda j: (0, j)),
        out_shape=jax.ShapeDtypeStruct((bsz, n), F32),
        compiler_params=_params(("arbitrary",)),
        name="adaln_mod",
    )(c, w_ada, b_ada.reshape(1, n))


def _taps(ext, w_ref, lo, r0):
    taps = w_ref.shape[0]
    out = w_ref[taps - 1:taps, lo:lo + STRIP] * ext[HALO + r0:HALO + r0 + ROWS, :]
    for k in range(taps - 1):
        back = taps - 1 - k
        out = out + w_ref[k:k + 1, lo:lo + STRIP] * ext[HALO + r0 - back:HALO + r0 - back + ROWS, :]
    return out


def _inproj_kernel(x_ref, mod_ref, g_ref, w_ref, wdt_ref, wsc_ref, wcv_ref, bcv_ref, dtb_ref, o_ref, dt_ref,
                   hx_scr, hu_scr, acc, *, tiles_per_seq):
    tm = x_ref.shape[0]

    @pl.when(pl.program_id(0) % tiles_per_seq == 0)
    def _():
        hx_scr[...] = jnp.zeros_like(hx_scr)
        hu_scr[...] = jnp.zeros_like(hu_scr)

    sh = mod_ref[0, 0:1, :]
    sc = mod_ref[0, 1:2, :]
    h = (_rms(x_ref[...], g_ref[...]) * (1.0 + sc) + sh).astype(BF16)
    dt_ref[...] = jax.nn.softplus(jnp.dot(h, wdt_ref[...], preferred_element_type=F32) + dtb_ref[...])

    def project(buf, w_lo):
        acc[buf, HALO:HALO + tm, :] = jnp.dot(h, w_ref[:, w_lo:w_lo + STRIP], preferred_element_type=F32)

    def pointwise(buf, fn, o_lo):
        for r0 in range(0, tm, ROWS):
            o_ref[r0:r0 + ROWS, o_lo:o_lo + STRIP] = fn(acc[buf, HALO + r0:HALO + r0 + ROWS, :]).astype(BF16)

    def conv_b(buf, lo):
        ext = acc.at[buf]
        ext[0:HALO, :] = hx_scr[:, lo:lo + STRIP]
        for r0 in range(0, tm, ROWS):
            cv = _taps(ext, wcv_ref, lo, r0) + bcv_ref[:, lo:lo + STRIP]
            o_ref[r0:r0 + ROWS, P_ACT + lo:P_ACT + lo + STRIP] = _silu(cv).astype(BF16)
        hx_scr[:, lo:lo + STRIP] = ext[tm:tm + HALO, :]

    def conv_a(b_sb, b_sx, b_sc, lo):
        ext = acc.at[b_sb]
        ext[0:HALO, :] = hu_scr[:, lo:lo + STRIP]
        for r0 in range(0, tm, ROWS):
            rs = slice(HALO + r0, HALO + r0 + ROWS)
            ext[rs, :] = ext[rs, :] * acc[b_sx, rs, :]
        for r0 in range(0, tm, ROWS):
            rs = slice(HALO + r0, HALO + r0 + ROWS)
            o_ref[r0:r0 + ROWS, P_YA + lo:P_YA + lo + STRIP] = (
                acc[b_sc, rs, :] * _taps(ext, wsc_ref, lo, r0)).astype(BF16)
        hu_scr[:, lo:lo + STRIP] = ext[tm:tm + HALO, :]

    work = []
    for j in range(D_INNER // STRIP):
        work.append(([W_Z + j * STRIP], lambda b, j=j: pointwise(b[0], _silu, P_SZ + j * STRIP)))
    for j in range(D_XBC // STRIP):
        work.append(([W_XBC + j * STRIP], lambda b, j=j: conv_b(b[0], j * STRIP)))
    for j in range(D_MODEL // STRIP):
        work.append(([W_SB + j * STRIP, W_SX + j * STRIP, W_SC + j * STRIP],
                     lambda b, j=j: conv_a(b[0], b[1], b[2], j * STRIP)))
    for j in range(2 * D_MODEL // STRIP):
        work.append(([W_GA + j * STRIP], lambda b, j=j: pointwise(b[0], jax.nn.sigmoid, P_GA + j * STRIP)))
    nxt = 0
    pending = None
    for w_los, finish in work:
        bufs = [(nxt + i) % N_ACC for i in range(len(w_los))]
        nxt += len(w_los)
        for b, w_lo in zip(bufs, w_los):
            project(b, w_lo)
        if pending is not None:
            pending()
        pending = functools.partial(finish, bufs)
    pending()


def _in_projection(x2d, mod, g_mix, w_main, w_dt, w_sconv, w_ssm_conv, b_ssm_conv, dt_bias, seq):
    t, d = x2d.shape
    tm = min(512, seq)
    per_b = seq // tm
    assert D_MODEL // STRIP == 1 or N_ACC >= 6
    dtb = jnp.pad(dt_bias, (0, LANES - SSM_HEADS)).reshape(1, LANES)
    const = lambda shape: pl.BlockSpec(shape, lambda i: (0,) * len(shape))
    return pl.pallas_call(
        functools.partial(_inproj_kernel, tiles_per_seq=per_b),
        grid=(t // tm,),
        in_specs=[pl.BlockSpec((tm, d), lambda i: (i, 0)),
                  pl.BlockSpec((1, N_MOD, d), lambda i: (i // per_b, 0, 0)),
                  const((1, d)), const((d, W_TOTAL)), const((d, LANES)),
                  const((CONV_K, D_MODEL)), const((SSM_CONV_K, D_XBC)), const((1, D_XBC)), const((1, LANES))],
        out_specs=[pl.BlockSpec((tm, P_TOTAL), lambda i: (i, 0)),
                   pl.BlockSpec((tm, LANES), lambda i: (i, 0))],
        out_shape=[jax.ShapeDtypeStruct((t, P_TOTAL), BF16),
                   jax.ShapeDtypeStruct((t, LANES), F32)],
        scratch_shapes=[pltpu.VMEM((HALO, D_XBC), F32), pltpu.VMEM((HALO, D_MODEL), F32),
                        pltpu.VMEM((N_ACC, tm + HALO, STRIP), F32)],
        compiler_params=_params(("arbitrary",)),
        name="in_proj",
    )(x2d, mod, g_mix.reshape(1, d), w_main, w_dt, w_sconv, w_ssm_conv, b_ssm_conv.reshape(1, D_XBC), dtb)


def _split3(v):
    hi = v.astype(BF16)
    r1 = v - hi.astype(F32)
    mid = r1.astype(BF16)
    lo = (r1 - mid.astype(F32)).astype(BF16)
    return hi, mid, lo


def _mixer_kernel(proj_ref, dt_ref, x_ref, mod_ref, alog_ref, dskip_ref, gn_ref, e_ref, wa_ref, wb_ref, wo_ref,
                  o_ref, s_scr):
    L = SSM_CHUNK

    @pl.when(pl.program_id(1) == 0)
    def _():
        s_scr[...] = jnp.zeros_like(s_scr)

    def col(lo, width):
        return proj_ref[:, lo:lo + width].astype(F32)

    y_a = jnp.dot(proj_ref[:, P_YA:P_YA + D_MODEL], wa_ref[...], preferred_element_type=F32)
    xs_b = proj_ref[:, P_ACT:P_ACT + D_INNER]
    xs = xs_b.astype(F32)
    bs_lo = P_ACT + D_INNER
    cs_lo = bs_lo + SSM_GROUPS * SSM_STATE

    dt = dt_ref[...]
    a = -jnp.exp(alog_ref[...])
    da = dt * a
    row = lax.broadcasted_iota(jnp.int32, (L, L), 0)
    cidx = lax.broadcasted_iota(jnp.int32, (L, L), 1)
    causal = cidx <= row
    tri = jnp.where(causal, 1.0, 0.0).astype(BF16)
    a_cum = sum(jnp.dot(tri, p, preferred_element_type=F32) for p in _split3(da))
    a_cum_t = a_cum.T
    ea = jnp.exp(a_cum)
    de = jnp.exp(a_cum[L - 1:L, :] - a_cum)
    e_mat = e_ref[...]
    dt_e = jnp.dot(dt.astype(BF16), e_mat, preferred_element_type=F32).astype(BF16)
    w_e = jnp.dot((dt * de).astype(BF16), e_mat, preferred_element_type=F32).astype(BF16)
    ea_e = jnp.dot(ea.astype(BF16), e_mat, preferred_element_type=F32)
    xdt = xs_b * dt_e
    xw = xs_b * w_e
    bs_t = col(bs_lo, SSM_GROUPS * SSM_STATE).T.astype(BF16)
    lane = lax.broadcasted_iota(jnp.int32, (L, 2 * SSM_HEAD_DIM), 1)
    lo_half = lane < SSM_HEAD_DIM

    y_groups = []
    for g in range(SSM_GROUPS):
        cg = proj_ref[:, cs_lo + g * SSM_STATE:cs_lo + (g + 1) * SSM_STATE]
        bg = proj_ref[:, bs_lo + g * SSM_STATE:bs_lo + (g + 1) * SSM_STATE]
        bg_t = bs_t[g * SSM_STATE:(g + 1) * SSM_STATE, :]
        cb = lax.dot_general(cg, bg, (((1,), (1,)), ((), ())), preferred_element_type=F32)
        s_old = s_scr[g]
        gsl = slice(g * GROUP_WIDTH, (g + 1) * GROUP_WIDTH)
        y_off = jnp.dot(cg, s_old.astype(BF16), preferred_element_type=F32) * ea_e[:, gsl]
        pieces = []
        for p in range(HEADS_PER_GROUP // 2):
            acc = None
            xpair = xdt[:, g * GROUP_WIDTH + p * 2 * SSM_HEAD_DIM:g * GROUP_WIDTH + (p + 1) * 2 * SSM_HEAD_DIM]
            for half in range(2):
                h = g * HEADS_PER_GROUP + 2 * p + half
                seg = a_cum[:, h:h + 1] - a_cum_t[h:h + 1, :]
                decay = jnp.exp(jnp.where(causal, seg, -jnp.inf))
                scores = (cb * decay).astype(BF16)
                keep = lo_half if half == 0 else jnp.logical_not(lo_half)
                part = jnp.dot(scores, jnp.where(keep, xpair, jnp.zeros_like(xpair)),
                               preferred_element_type=F32)
                acc = part if acc is None else acc + part
            pieces.append(acc)
        y_groups.append(jnp.concatenate(pieces, axis=1) + y_off)
        s_new = s_old * ea_e[L - 1:L, gsl] + jnp.dot(bg_t, xw[:, gsl], preferred_element_type=F32)
        s_scr[g] = s_new
    y = jnp.concatenate(y_groups, axis=1) + dskip_ref[...] * xs

    yf = y * col(P_SZ, D_INNER)
    normed = []
    for g in range(SSM_GROUPS):
        blk = yf[:, g * GROUP_WIDTH:(g + 1) * GROUP_WIDTH]
        normed.append(blk * lax.rsqrt(jnp.mean(blk * blk, axis=-1, keepdims=True) + EPS))
    yn = (jnp.concatenate(normed, axis=1) * gn_ref[...]).astype(BF16)
    y_b = jnp.dot(yn, wb_ref[...], preferred_element_type=F32)
    m = col(P_GA, D_MODEL) * y_a + col(P_GB, D_MODEL) * y_b
    mix = jnp.dot(m.astype(BF16), wo_ref[...], preferred_element_type=F32)
    o_ref[...] = x_ref[...] + mod_ref[0, 2:3, :] * mix


def _token_mixer(proj, dt, x2d, mod, a_log, d_skip, g_ssm_norm, w_a, w_b, w_o, bsz, seq):
    t, d = x2d.shape
    L = SSM_CHUNK
    nc = seq // L
    alog = jnp.pad(a_log, (0, LANES - SSM_HEADS)).reshape(1, LANES)
    dskip_e = jnp.repeat(d_skip, SSM_HEAD_DIM).reshape(1, D_INNER)
    expand = (jnp.arange(D_INNER)[None, :] // SSM_HEAD_DIM == jnp.arange(LANES)[:, None]).astype(BF16)
    full = lambda shape: pl.BlockSpec(shape, lambda b, c: (0,) * len(shape))
    return pl.pallas_call(
        _mixer_kernel,
        grid=(bsz, nc),
        in_specs=[pl.BlockSpec((L, P_TOTAL), lambda b, c: (b * nc + c, 0)),
                  pl.BlockSpec((L, LANES), lambda b, c: (b * nc + c, 0)),
                  pl.BlockSpec((L, d), lambda b, c: (b * nc + c, 0)),
                  pl.BlockSpec((1, N_MOD, d), lambda b, c: (b, 0, 0)),
                  full((1, LANES)), full((1, D_INNER)), full((1, D_INNER)), full((LANES, D_INNER)),
                  full((D_MODEL, D_MODEL)), full((D_INNER, D_MODEL)), full((D_MODEL, D_MODEL))],
        out_specs=pl.BlockSpec((L, d), lambda b, c: (b * nc + c, 0)),
        out_shape=jax.ShapeDtypeStruct((t, d), F32),
        scratch_shapes=[pltpu.VMEM((SSM_GROUPS, SSM_STATE, GROUP_WIDTH), F32)],
        compiler_params=_params(("arbitrary", "arbitrary")),
        name="token_mixer",
    )(proj, dt, x2d, mod, alog, dskip_e, g_ssm_norm.reshape(1, D_INNER), expand, w_a, w_b, w_o)


def _ffn_input(x, mod_ref, g_ref):
    return _rms(x, g_ref[...]) * (1.0 + mod_ref[0, 4:5, :]) + mod_ref[0, 3:4, :]


def _router_kernel(x_ref, mod_ref, g_ref, wr_ref, br_ref, pos_ref, post_ref, w_ref, cnt_ref):
    tm = x_ref.shape[0]
    h = _ffn_input(x_ref[...], mod_ref, g_ref)
    logits = jnp.dot(h, wr_ref[...], preferred_element_type=F32,
                     precision=lax.Precision.HIGHEST) + br_ref[...]
    lane = lax.broadcasted_iota(jnp.int32, (tm, LANES), 1)
    work = jnp.where(lane < N_EXPERTS, logits, -jnp.inf)
    vals, hots = [], []
    for _ in range(TOP_K):
        v = jnp.max(work, axis=-1, keepdims=True)
        i = jnp.min(jnp.where(work == v, lane, LANES), axis=-1, keepdims=True)
        hot = lane == i
        vals.append(v)
        hots.append(hot)
        work = jnp.where(hot, -jnp.inf, work)
    exps = [jnp.exp(v - vals[0]) for v in vals]
    denom = exps[0] + exps[1] + exps[2] + exps[3]

    onehot = jnp.zeros((tm, LANES), F32)
    for hot in hots:
        onehot = onehot + jnp.where(hot, 1.0, 0.0)
    r = lax.broadcasted_iota(jnp.int32, (tm, tm), 0)
    c = lax.broadcasted_iota(jnp.int32, (tm, tm), 1)
    strict = jnp.where(c < r, 1.0, 0.0).astype(BF16)
    before = jnp.dot(strict, onehot.astype(BF16), preferred_element_type=F32)
    cnt = jnp.sum(onehot, axis=0, keepdims=True).astype(jnp.int32)
    seg_units = (cnt + (SEG_ALIGN - 1)) // SEG_ALIGN
    er = lax.broadcasted_iota(jnp.int32, (LANES, LANES), 0)
    ec = lax.broadcasted_iota(jnp.int32, (LANES, LANES), 1)
    upper = jnp.where(er < ec, 1.0, 0.0).astype(BF16)
    units8 = jnp.broadcast_to(seg_units.astype(F32), (8, LANES)).astype(BF16)
    seg_off = jnp.dot(units8, upper, preferred_element_type=F32)[0:1, :] * float(SEG_ALIGN)
    where_to = before + seg_off
    packed = jnp.zeros((tm, LANES), F32)
    for k in range(TOP_K):
        pos = jnp.sum(jnp.where(hots[k], where_to, 0.0), axis=-1, keepdims=True)
        pos_ref[:, k:k + 1] = pos.astype(jnp.int32)
        w_ref[:, k:k + 1] = exps[k] / denom
        packed = jnp.where(lane == k, pos, packed)
    post_ref[...] = packed.T[0:8, :].astype(jnp.int32)
    cnt_ref[0] = cnt


def _router(x1, mod, g_ffn, w_router, b_router, seq):
    t, d = x1.shape
    tm = MOE_TILE
    per_b = seq // tm
    wr = jnp.pad(w_router, ((0, 0), (0, LANES - N_EXPERTS)))
    br = jnp.pad(b_router, (0, LANES - N_EXPERTS)).reshape(1, LANES)
    slot_spec = pl.BlockSpec((tm, TOP_K), lambda i: (i, 0))
    return pl.pallas_call(
        _router_kernel,
        grid=(t // tm,),
        in_specs=[pl.BlockSpec((tm, d), lambda i: (i, 0)),
                  pl.BlockSpec((1, N_MOD, d), lambda i: (i // per_b, 0, 0)),
                  pl.BlockSpec((1, d), lambda i: (0, 0)),
                  pl.BlockSpec((d, LANES), lambda i: (0, 0)),
                  pl.BlockSpec((1, LANES), lambda i: (0, 0))],
        out_specs=[slot_spec, pl.BlockSpec((8, tm), lambda i: (0, i)), slot_spec,
                   pl.BlockSpec((1, 1, LANES), lambda i: (i, 0, 0))],
        out_shape=[jax.ShapeDtypeStruct((t, TOP_K), jnp.int32),
                   jax.ShapeDtypeStruct((8, t), jnp.int32),
                   jax.ShapeDtypeStruct((t, TOP_K), F32),
                   jax.ShapeDtypeStruct((t // tm, 1, LANES), jnp.int32)],
        compiler_params=_params(("arbitrary",)),
        name="router",
    )(x1, mod, g_ffn.reshape(1, d), wr, br)


def _segment_copies(tile, len_ref, loc_ref, glob_ref, make_copy, wait):
    if wait:
        end = tile * N_EXPERTS + N_EXPERTS - 1
        rows = loc_ref[end] + len_ref[end]
        make_copy(0, 0, pl.multiple_of(rows, SEG_ALIGN)).wait()
        return

    def per_expert(e, carry):
        j = tile * N_EXPERTS + e
        ln, lo, go = len_ref[j], loc_ref[j], glob_ref[j]

        @pl.when(ln > 0)
        def _():
            make_copy(pl.multiple_of(lo, SEG_ALIGN), pl.multiple_of(go, SEG_ALIGN),
                      pl.multiple_of(ln, SEG_ALIGN)).start()
        return carry

    lax.fori_loop(0, N_EXPERTS, per_expert, 0)


def _dispatch_kernel(len_ref, loc_ref, glob_ref, tail_ref, post_ref, x_ref, mod_ref, g_ref, xs_ref,
                     sort_scr, zero_scr, sem):
    i = pl.program_id(0)
    last = pl.num_programs(0) - 1
    slot = i % 2
    tm = x_ref.shape[0]
    r_max = sort_scr.shape[1]

    @pl.when(i == 0)
    def _():
        zero_scr[...] = jnp.zeros_like(zero_scr)
        for wait in (False, True):
            for e in range(N_EXPERTS):
                @pl.when(tail_ref[e] >= 0)
                def _():
                    cp = pltpu.make_async_copy(
                        zero_scr, xs_ref.at[pl.ds(pl.multiple_of(tail_ref[e], SEG_ALIGN), MOE_BLOCK), :],
                        sem.at[0])
                    cp.wait() if wait else cp.start()

    h = _ffn_input(x_ref[...], mod_ref, g_ref).astype(BF16)
    rows = lax.broadcasted_iota(jnp.int32, (r_max, tm), 0)
    perm = jnp.zeros((r_max, tm), F32)
    for k in range(TOP_K):
        perm = jnp.where(rows == post_ref[k:k + 1, :], 1.0, perm)
    sort_scr[slot] = jnp.dot(perm.astype(BF16), h, preferred_element_type=F32).astype(BF16)

    def copies_from(buf):
        def make_copy(lo, go, n):
            return pltpu.make_async_copy(sort_scr.at[buf, pl.ds(lo, n), :], xs_ref.at[pl.ds(go, n), :],
                                         sem.at[buf])
        return make_copy

    _segment_copies(i, len_ref, loc_ref, glob_ref, copies_from(slot), wait=False)

    @pl.when(i > 0)
    def _():
        _segment_copies(i - 1, len_ref, loc_ref, glob_ref, copies_from(1 - slot), wait=True)

    @pl.when(i == last)
    def _():
        _segment_copies(i, len_ref, loc_ref, glob_ref, copies_from(slot), wait=True)


def _dispatch(seg_len, seg_loc, seg_glob, tail, pos_t, x1, mod, g_ffn, r_tot, seq):
    t, d = x1.shape
    tm = MOE_TILE
    per_b = seq // tm
    return pl.pallas_call(
        _dispatch_kernel,
        grid_spec=pltpu.PrefetchScalarGridSpec(
            num_scalar_prefetch=4, grid=(t // tm,),
            in_specs=[pl.BlockSpec((8, tm), lambda i, *_: (0, i)),
                      pl.BlockSpec((tm, d), lambda i, *_: (i, 0)),
                      pl.BlockSpec((1, N_MOD, d), lambda i, *_: (i // per_b, 0, 0)),
                      pl.BlockSpec((1, d), lambda i, *_: (0, 0))],
            out_specs=pl.BlockSpec(memory_space=pl.ANY),
            scratch_shapes=[pltpu.VMEM((2, SORT_ROWS, d), BF16), pltpu.VMEM((MOE_BLOCK, d), BF16),
                            pltpu.SemaphoreType.DMA((2,))]),
        out_shape=jax.ShapeDtypeStruct((r_tot, d), BF16),
        compiler_params=_params(("arbitrary",)),
        name="dispatch",
    )(seg_len, seg_loc, seg_glob, tail, pos_t, x1, mod, g_ffn.reshape(1, d))


def _expert_kernel(be_ref, nu_ref, x_ref, wgu_ref, bgu_ref, wd_ref, bd_ref, o_ref):
    @pl.when(pl.program_id(0) < nu_ref[0])
    def _():
        gu = jnp.dot(x_ref[...], wgu_ref[0], preferred_element_type=F32) + bgu_ref[0]
        gate = jnp.minimum(gu[:, :D_FF], SWIGLU_LIMIT)
        up = jnp.clip(gu[:, D_FF:], -SWIGLU_LIMIT, SWIGLU_LIMIT)
        act = (up + 1.0) * gate * jax.nn.sigmoid(SWIGLU_ALPHA * gate)
        y = jnp.dot(act.astype(BF16), wd_ref[0], preferred_element_type=F32) + bd_ref[0]
        o_ref[...] = y.astype(BF16)


def _expert_blocks(block_e, n_used, xs, w_gu, b_gu, w_down, b_down):
    r_tot, d = xs.shape
    n_blocks = r_tot // MOE_BLOCK
    row_map = lambda i, be, nu: (jnp.minimum(i, nu[0] - 1), 0)
    e_map = lambda i, be, nu: (be[i], 0, 0)
    return pl.pallas_call(
        _expert_kernel,
        grid_spec=pltpu.PrefetchScalarGridSpec(
            num_scalar_prefetch=2, grid=(n_blocks,),
            in_specs=[pl.BlockSpec((MOE_BLOCK, d), row_map),
                      pl.BlockSpec((1, d, 2 * D_FF), e_map),
                      pl.BlockSpec((1, 1, 2 * D_FF), e_map),
                      pl.BlockSpec((1, D_FF, d), e_map),
                      pl.BlockSpec((1, 1, d), e_map)],
            out_specs=pl.BlockSpec((MOE_BLOCK, d), row_map)),
        out_shape=jax.ShapeDtypeStruct((r_tot, d), BF16),
        compiler_params=_params(("arbitrary",)),
        name="expert_blocks",
    )(block_e, n_used, xs, w_gu, b_gu.reshape(N_EXPERTS, 1, 2 * D_FF), w_down,
      b_down.reshape(N_EXPERTS, 1, d))


def _combine_kernel(len_ref, loc_ref, glob_ref, pos_ref, w_ref, x_ref, mod_ref, g_ref, y_ref, o_ref,
                    sort_scr, sem, *, final_norm):
    i = pl.program_id(0)
    last = pl.num_programs(0) - 1
    slot = i % 2
    tm = x_ref.shape[0]
    r_max = sort_scr.shape[1]

    def copies_into(buf):
        def make_copy(lo, go, n):
            return pltpu.make_async_copy(y_ref.at[pl.ds(go, n), :], sort_scr.at[buf, pl.ds(lo, n), :],
                                         sem.at[buf])
        return make_copy

    def fetch(tile, buf):
        sort_scr[buf] = jnp.zeros(sort_scr.shape[1:], sort_scr.dtype)
        _segment_copies(tile, len_ref, loc_ref, glob_ref, copies_into(buf), wait=False)

    @pl.when(i == 0)
    def _():
        fetch(i, slot)

    @pl.when(i < last)
    def _():
        fetch(i + 1, 1 - slot)

    _segment_copies(i, len_ref, loc_ref, glob_ref, copies_into(slot), wait=True)

    cols = lax.broadcasted_iota(jnp.int32, (tm, r_max), 1)
    w = w_ref[...]
    pos = pos_ref[...]
    gather_w = jnp.zeros((tm, r_max), F32)
    for k in range(TOP_K):
        gather_w = jnp.where(cols == pos[:, k:k + 1], w[:, k:k + 1], gather_w)
    moe = jnp.dot(gather_w.astype(BF16), sort_scr[slot], preferred_element_type=F32)
    x2 = x_ref[...] + mod_ref[0, 5:6, :] * moe
    o_ref[...] = _rms(x2, g_ref[...]) if final_norm else x2


def _combine(seg_len, seg_loc, seg_glob, pos, top_w, x1, mod, g_final, y, seq, final_norm):
    t, d = x1.shape
    tm = MOE_TILE
    per_b = seq // tm
    slot_spec = pl.BlockSpec((tm, TOP_K), lambda i, *_: (i, 0))
    return pl.pallas_call(
        functools.partial(_combine_kernel, final_norm=final_norm),
        grid_spec=pltpu.PrefetchScalarGridSpec(
            num_scalar_prefetch=3, grid=(t // tm,),
            in_specs=[slot_spec, slot_spec,
                      pl.BlockSpec((tm, d), lambda i, *_: (i, 0)),
                      pl.BlockSpec((1, N_MOD, d), lambda i, *_: (i // per_b, 0, 0)),
                      pl.BlockSpec((1, d), lambda i, *_: (0, 0)),
                      pl.BlockSpec(memory_space=pl.ANY)],
            out_specs=pl.BlockSpec((tm, d), lambda i, *_: (i, 0)),
            scratch_shapes=[pltpu.VMEM((2, SORT_ROWS, d), BF16), pltpu.SemaphoreType.DMA((2,))]),
        out_shape=jax.ShapeDtypeStruct((t, d), F32),
        compiler_params=_params(("arbitrary",)),
        name="combine",
    )(seg_len, seg_loc, seg_glob, pos, top_w, x1, mod, g_final.reshape(1, d), y)


def _regroup_w_in(w_in):
    o_sb, o_sc, o_sx = 0, D_MODEL, 2 * D_MODEL
    o_z = 3 * D_MODEL
    o_xbc = o_z + D_INNER
    o_dt = o_xbc + D_XBC
    o_ga = o_dt + SSM_HEADS
    o_gb = o_ga + D_MODEL
    main = jnp.concatenate([w_in[:, o_z:o_xbc], w_in[:, o_xbc:o_dt], w_in[:, o_sb:o_sc], w_in[:, o_sc:o_sx],
                            w_in[:, o_sx:o_z], w_in[:, o_ga:o_gb], w_in[:, o_gb:o_gb + D_MODEL]], axis=1)
    w_dt = jnp.pad(w_in[:, o_dt:o_ga], ((0, 0), (0, LANES - SSM_HEADS)))
    return main.astype(BF16), w_dt.astype(BF16)


def _segment_tables(counts, n_blocks):
    seg_len = (counts + SEG_ALIGN - 1) // SEG_ALIGN * SEG_ALIGN
    seg_loc = jnp.cumsum(seg_len, axis=1) - seg_len
    used = jnp.sum(seg_len, axis=0)
    padded = (used + MOE_BLOCK - 1) // MOE_BLOCK * MOE_BLOCK
    pend = jnp.cumsum(padded)
    pstart = pend - padded
    seg_glob = pstart[None, :] + jnp.cumsum(seg_len, axis=0) - seg_len
    tail = jnp.where(padded > 0, pend - MOE_BLOCK, -1)
    first_row = jnp.arange(n_blocks, dtype=jnp.int32) * MOE_BLOCK
    block_e = jnp.sum((pend[None, :] <= first_row[:, None]).astype(jnp.int32), axis=1)
    block_e = jnp.minimum(block_e, N_EXPERTS - 1)
    n_used = pend[-1:] // MOE_BLOCK
    i32 = lambda a: a.astype(jnp.int32)
    return (i32(seg_len).reshape(-1), i32(seg_loc).reshape(-1), i32(seg_glob).reshape(-1), i32(tail),
            i32(block_e), i32(n_used))


def kernel(x, c, w_ada, b_ada, g_mix, w_in, w_sconv, w_sconv_out, w_ssm_conv, b_ssm_conv, dt_bias, a_log,
           d_skip, g_ssm_norm, w_ssm_out, w_o, g_ffn, w_router, b_router, w_gu, b_gu, w_down, b_down,
           g_final):
    bsz, seq, d = x.shape
    depth = w_ada.shape[0]
    t = bsz * seq
    assert seq % MOE_TILE == 0 and seq % SSM_CHUNK == 0
    n_tiles = t // MOE_TILE
    worst_rows = t * TOP_K + n_tiles * N_EXPERTS * (SEG_ALIGN - 1) + N_EXPERTS * (MOE_BLOCK - 1)
    n_blocks = -(-worst_rows // MOE_BLOCK)
    r_tot = n_blocks * MOE_BLOCK
    xf = x.reshape(t, d)
    for l in range(depth):
        mod = _modulation(c, w_ada[l], b_ada[l]).reshape(bsz, N_MOD, d)
        w_main, w_dt = _regroup_w_in(w_in[l])
        proj, dt = _in_projection(xf, mod, g_mix[l], w_main, w_dt, w_sconv[l], w_ssm_conv[l], b_ssm_conv[l],
                                  dt_bias[l], seq)
        x1 = _token_mixer(proj, dt, xf, mod, a_log[l], d_skip[l], g_ssm_norm[l], w_sconv_out[l].astype(BF16),
                          w_ssm_out[l].astype(BF16), w_o[l].astype(BF16), bsz, seq)
        pos, pos_t, top_w, counts = _router(x1, mod, g_ffn[l], w_router[l], b_router[l], seq)
        seg_len, seg_loc, seg_glob, tail, block_e, n_used = _segment_tables(
            counts[:, 0, :N_EXPERTS], n_blocks)
        xs = _dispatch(seg_len, seg_loc, seg_glob, tail, pos_t, x1, mod, g_ffn[l], r_tot, seq)
        ys = _expert_blocks(block_e, n_used, xs, w_gu[l].astype(BF16), b_gu[l], w_down[l].astype(BF16),
                            b_down[l])
        xf = _combine(seg_len, seg_loc, seg_glob, pos, top_w, x1, mod, g_final, ys, seq,
                      final_norm=(l == depth - 1))
    return xf.reshape(bsz, seq, d)
```

```python
import functools

import jax
import jax.numpy as jnp
from jax import lax
from jax.experimental import pallas as pl
from jax.experimental.pallas import tpu as pltpu

F32 = jnp.float32
BF16 = jnp.bfloat16

D_MODEL = 1024
CONV_K = 3
D_INNER = 2048
SSM_HEAD_DIM = 64
SSM_HEADS = 32
SSM_GROUPS = 4
SSM_STATE = 128
SSM_CONV_K = 4
SSM_CHUNK = 256
D_XBC = D_INNER + 2 * SSM_GROUPS * SSM_STATE
N_EXPERTS = 32
TOP_K = 4
D_FF = D_MODEL
SWIGLU_LIMIT = 7.0
SWIGLU_ALPHA = 1.702
EPS = 1e-6
N_MOD = 6

HEADS_PER_GROUP = SSM_HEADS // SSM_GROUPS
GROUP_WIDTH = D_INNER // SSM_GROUPS
LANES = 128
HALO = 8

MOE_BLOCK = 512
MOE_TILE = 512
SEG_ALIGN = 16
MXU_DIM = 256
SORT_ROWS = -(-(MOE_TILE * TOP_K + N_EXPERTS * (SEG_ALIGN - 1)) // MXU_DIM) * MXU_DIM

W_Z = 0
W_XBC = W_Z + D_INNER
W_SB = W_XBC + D_XBC
W_SC = W_SB + D_MODEL
W_SX = W_SC + D_MODEL
W_GA = W_SX + D_MODEL
W_GB = W_GA + D_MODEL
W_TOTAL = W_GB + D_MODEL

P_SZ = 0
P_ACT = P_SZ + D_INNER
P_YA = P_ACT + D_XBC
P_GA = P_YA + D_MODEL
P_GB = P_GA + D_MODEL
P_TOTAL = P_GB + D_MODEL
STRIP = 1024
ROWS = 128
N_ACC = 4

VMEM_LIMIT = 56 * 1024 * 1024


def _params(sem):
    return pltpu.CompilerParams(dimension_semantics=sem, vmem_limit_bytes=VMEM_LIMIT)


def _rms(x, g):
    return x * lax.rsqrt(jnp.mean(x * x, axis=-1, keepdims=True) + EPS) * g


def _silu(x):
    return x * jax.nn.sigmoid(x)


def _mod_kernel(c_ref, w_ref, b_ref, o_ref):
    c = c_ref[...]
    cond = _silu(c)
    o_ref[...] = jnp.dot(cond, w_ref[...], preferred_element_type=F32,
                         precision=lax.Precision.HIGHEST) + b_ref[...]


def _modulation(c, w_ada, b_ada):
    bsz, d = c.shape
    n = w_ada.shape[1]
    bn = 1024
    return pl.pallas_call(
        _mod_kernel,
        grid=(n // bn,),
        in_specs=[pl.BlockSpec((bsz, d), lambda j: (0, 0)),
                  pl.BlockSpec((d, bn), lambda j: (0, j)),
                  pl.BlockSpec((1, bn), lambda j: (0, j))],
        out_specs=pl.BlockSpec((bsz, bn), lambda j: (0, j)),
        out_shape=jax.ShapeDtypeStruct((bsz, n), F32),
        compiler_params=_params(("arbitrary",)),
        name="adaln_mod",
    )(c, w_ada, b_ada.reshape(1, n))


def _taps(ext, w_ref, lo, r0):
    taps = w_ref.shape[0]
    out = w_ref[taps - 1:taps, lo:lo + STRIP] * ext[HALO + r0:HALO + r0 + ROWS, :]
    for k in range(taps - 1):
        back = taps - 1 - k
        out = out + w_ref[k:k + 1, lo:lo + STRIP] * ext[HALO + r0 - back:HALO + r0 - back + ROWS, :]
    return out


def _inproj_kernel(x_ref, mod_ref, g_ref, w_ref, wdt_ref, wsc_ref, wcv_ref, bcv_ref, dtb_ref, o_ref, dt_ref,
                   hx_scr, hu_scr, acc, *, tiles_per_seq):
    tm = x_ref.shape[0]

    @pl.when(pl.program_id(0) % tiles_per_seq == 0)
    def _():
        hx_scr[...] = jnp.zeros_like(hx_scr)
        hu_scr[...] = jnp.zeros_like(hu_scr)

    sh = mod_ref[0, 0:1, :]
    sc = mod_ref[0, 1:2, :]
    h = (_rms(x_ref[...], g_ref[...]) * (1.0 + sc) + sh).astype(BF16)
    dt_ref[...] = jax.nn.softplus(jnp.dot(h, wdt_ref[...], preferred_element_type=F32) + dtb_ref[...])

    def project(buf, w_lo):
        acc[buf, HALO:HALO + tm, :] = jnp.dot(h, w_ref[:, w_lo:w_lo + STRIP], preferred_element_type=F32)

    def pointwise(buf, fn, o_lo):
        for r0 in range(0, tm, ROWS):
            o_ref[r0:r0 + ROWS, o_lo:o_lo + STRIP] = fn(acc[buf, HALO + r0:HALO + r0 + ROWS, :]).astype(BF16)

    def conv_b(buf, lo):
        ext = acc.at[buf]
        ext[0:HALO, :] = hx_scr[:, lo:lo + STRIP]
        for r0 in range(0, tm, ROWS):
            cv = _taps(ext, wcv_ref, lo, r0) + bcv_ref[:, lo:lo + STRIP]
            o_ref[r0:r0 + ROWS, P_ACT + lo:P_ACT + lo + STRIP] = _silu(cv).astype(BF16)
        hx_scr[:, lo:lo + STRIP] = ext[tm:tm + HALO, :]

    def conv_a(b_sb, b_sx, b_sc, lo):
        ext = acc.at[b_sb]
        ext[0:HALO, :] = hu_scr[:, lo:lo + STRIP]
        for r0 in range(0, tm, ROWS):
            rs = slice(HALO + r0, HALO + r0 + ROWS)
            ext[rs, :] = ext[rs, :] * acc[b_sx, rs, :]
        for r0 in range(0, tm, ROWS):
            rs = slice(HALO + r0, HALO + r0 + ROWS)
            o_ref[r0:r0 + ROWS, P_YA + lo:P_YA + lo + STRIP] = (
                acc[b_sc, rs, :] * _taps(ext, wsc_ref, lo, r0)).astype(BF16)
        hu_scr[:, lo:lo + STRIP] = ext[tm:tm + HALO, :]

    work = []
    for j in range(D_INNER // STRIP):
        work.append(([W_Z + j * STRIP], lambda b, j=j: pointwise(b[0], _silu, P_SZ + j * STRIP)))
    for j in range(D_XBC // STRIP):
        work.append(([W_XBC + j * STRIP], lambda b, j=j: conv_b(b[0], j * STRIP)))
    for j in range(D_MODEL // STRIP):
        work.append(([W_SB + j * STRIP, W_SX + j * STRIP, W_SC + j * STRIP],
                     lambda b, j=j: conv_a(b[0], b[1], b[2], j * STRIP)))
    for j in range(2 * D_MODEL // STRIP):
        work.append(([W_GA + j * STRIP], lambda b, j=j: pointwise(b[0], jax.nn.sigmoid, P_GA + j * STRIP)))
    nxt = 0
    pending = None
    for w_los, finish in work:
        bufs = [(nxt + i) % N_ACC for i in range(len(w_los))]
        nxt += len(w_los)
        for b, w_lo in zip(bufs, w_los):
            project(b, w_lo)
        if pending is not None:
            pending()
        pending = functools.partial(finish, bufs)
    pending()


def _in_projection(x2d, mod, g_mix, w_main, w_dt, w_sconv, w_ssm_conv, b_ssm_conv, dt_bias, seq):
    t, d = x2d.shape
    tm = min(512, seq)
    per_b = seq // tm
    assert D_MODEL // STRIP == 1 or N_ACC >= 6
    dtb = jnp.pad(dt_bias, (0, LANES - SSM_HEADS)).reshape(1, LANES)
    const = lambda shape: pl.BlockSpec(shape, lambda i: (0,) * len(shape))
    return pl.pallas_call(
        functools.partial(_inproj_kernel, tiles_per_seq=per_b),
        grid=(t // tm,),
        in_specs=[pl.BlockSpec((tm, d), lambda i: (i, 0)),
                  pl.BlockSpec((1, N_MOD, d), lambda i: (i // per_b, 0, 0)),
                  const((1, d)), const((d, W_TOTAL)), const((d, LANES)),
                  const((CONV_K, D_MODEL)), const((SSM_CONV_K, D_XBC)), const((1, D_XBC)), const((1, LANES))],
        out_specs=[pl.BlockSpec((tm, P_TOTAL), lambda i: (i, 0)),
                   pl.BlockSpec((tm, LANES), lambda i: (i, 0))],
        out_shape=[jax.ShapeDtypeStruct((t, P_TOTAL), BF16),
                   jax.ShapeDtypeStruct((t, LANES), F32)],
        scratch_shapes=[pltpu.VMEM((HALO, D_XBC), F32), pltpu.VMEM((HALO, D_MODEL), F32),
                        pltpu.VMEM((N_ACC, tm + HALO, STRIP), F32)],
        compiler_params=_params(("arbitrary",)),
        name="in_proj",
    )(x2d, mod, g_mix.reshape(1, d), w_main, w_dt, w_sconv, w_ssm_conv, b_ssm_conv.reshape(1, D_XBC), dtb)


def _split3(v):
    hi = v.astype(BF16)
    r1 = v - hi.astype(F32)
    mid = r1.astype(BF16)
    lo = (r1 - mid.astype(F32)).astype(BF16)
    return hi, mid, lo


def _mixer_kernel(proj_ref, dt_ref, x_ref, mod_ref, alog_ref, dskip_ref, gn_ref, e_ref, wa_ref, wb_ref, wo_ref,
                  o_ref, s_scr):
    L = SSM_CHUNK

    @pl.when(pl.program_id(1) == 0)
    def _():
        s_scr[...] = jnp.zeros_like(s_scr)

    def col(lo, width):
        return proj_ref[:, lo:lo + width].astype(F32)

    y_a = jnp.dot(proj_ref[:, P_YA:P_YA + D_MODEL], wa_ref[...], preferred_element_type=F32)
    xs_b = proj_ref[:, P_ACT:P_ACT + D_INNER]
    xs = xs_b.astype(F32)
    bs_lo = P_ACT + D_INNER
    cs_lo = bs_lo + SSM_GROUPS * SSM_STATE

    dt = dt_ref[...]
    a = -jnp.exp(alog_ref[...])
    da = dt * a
    row = lax.broadcasted_iota(jnp.int32, (L, L), 0)
    cidx = lax.broadcasted_iota(jnp.int32, (L, L), 1)
    causal = cidx <= row
    tri = jnp.where(causal, 1.0, 0.0).astype(BF16)
    a_cum = sum(jnp.dot(tri, p, preferred_element_type=F32) for p in _split3(da))
    a_cum_t = a_cum.T
    ea = jnp.exp(a_cum)
    de = jnp.exp(a_cum[L - 1:L, :] - a_cum)
    e_mat = e_ref[...]
    dt_e = jnp.dot(dt.astype(BF16), e_mat, preferred_element_type=F32).astype(BF16)
    w_e = jnp.dot((dt * de).astype(BF16), e_mat, preferred_element_type=F32).astype(BF16)
    ea_e = jnp.dot(ea.astype(BF16), e_mat, preferred_element_type=F32)
    xdt = xs_b * dt_e
    xw = xs_b * w_e
    bs_t = col(bs_lo, SSM_GROUPS * SSM_STATE).T.astype(BF16)
    lane = lax.broadcasted_iota(jnp.int32, (L, 2 * SSM_HEAD_DIM), 1)
    lo_half = lane < SSM_HEAD_DIM

    y_groups = []
    for g in range(SSM_GROUPS):
        cg = proj_ref[:, cs_lo + g * SSM_STATE:cs_lo + (g + 1) * SSM_STATE]
        bg = proj_ref[:, bs_lo + g * SSM_STATE:bs_lo + (g + 1) * SSM_STATE]
        bg_t = bs_t[g * SSM_STATE:(g + 1) * SSM_STATE, :]
        cb = lax.dot_general(cg, bg, (((1,), (1,)), ((), ())), preferred_element_type=F32)
        s_old = s_scr[g]
        gsl = slice(g * GROUP_WIDTH, (g + 1) * GROUP_WIDTH)
        y_off = jnp.dot(cg, s_old.astype(BF16), preferred_element_type=F32) * ea_e[:, gsl]
        pieces = []
        for p in range(HEADS_PER_GROUP // 2):
            acc = None
            xpair = xdt[:, g * GROUP_WIDTH + p * 2 * SSM_HEAD_DIM:g * GROUP_WIDTH + (p + 1) * 2 * SSM_HEAD_DIM]
            for half in range(2):
                h = g * HEADS_PER_GROUP + 2 * p + half
                seg = a_cum[:, h:h + 1] - a_cum_t[h:h + 1, :]
                decay = jnp.exp(jnp.where(causal, seg, -jnp.inf))
                scores = (cb * decay).astype(BF16)
                keep = lo_half if half == 0 else jnp.logical_not(lo_half)
                part = jnp.dot(scores, jnp.where(keep, xpair, jnp.zeros_like(xpair)),
                               preferred_element_type=F32)
                acc = part if acc is None else acc + part
            pieces.append(acc)
        y_groups.append(jnp.concatenate(pieces, axis=1) + y_off)
        s_new = s_old * ea_e[L - 1:L, gsl] + jnp.dot(bg_t, xw[:, gsl], preferred_element_type=F32)
        s_scr[g] = s_new
    y = jnp.concatenate(y_groups, axis=1) + dskip_ref[...] * xs

    yf = y * col(P_SZ, D_INNER)
    normed = []
    for g in range(SSM_GROUPS):
        blk = yf[:, g * GROUP_WIDTH:(g + 1) * GROUP_WIDTH]
        normed.append(blk * lax.rsqrt(jnp.mean(blk * blk, axis=-1, keepdims=True) + EPS))
    yn = (jnp.concatenate(normed, axis=1) * gn_ref[...]).astype(BF16)
    y_b = jnp.dot(yn, wb_ref[...], preferred_element_type=F32)
    m = col(P_GA, D_MODEL) * y_a + col(P_GB, D_MODEL) * y_b
    mix = jnp.dot(m.astype(BF16), wo_ref[...], preferred_element_type=F32)
    o_ref[...] = x_ref[...] + mod_ref[0, 2:3, :] * mix


def _token_mixer(proj, dt, x2d, mod, a_log, d_skip, g_ssm_norm, w_a, w_b, w_o, bsz, seq):
    t, d = x2d.shape
    L = SSM_CHUNK
    nc = seq // L
    alog = jnp.pad(a_log, (0, LANES - SSM_HEADS)).reshape(1, LANES)
    dskip_e = jnp.repeat(d_skip, SSM_HEAD_DIM).reshape(1, D_INNER)
    expand = (jnp.arange(D_INNER)[None, :] // SSM_HEAD_DIM == jnp.arange(LANES)[:, None]).astype(BF16)
    full = lambda shape: pl.BlockSpec(shape, lambda b, c: (0,) * len(shape))
    return pl.pallas_call(
        _mixer_kernel,
        grid=(bsz, nc),
        in_specs=[pl.BlockSpec((L, P_TOTAL), lambda b, c: (b * nc + c, 0)),
                  pl.BlockSpec((L, LANES), lambda b, c: (b * nc + c, 0)),
                  pl.BlockSpec((L, d), lambda b, c: (b * nc + c, 0)),
                  pl.BlockSpec((1, N_MOD, d), lambda b, c: (b, 0, 0)),
                  full((1, LANES)), full((1, D_INNER)), full((1, D_INNER)), full((LANES, D_INNER)),
                  full((D_MODEL, D_MODEL)), full((D_INNER, D_MODEL)), full((D_MODEL, D_MODEL))],
        out_specs=pl.BlockSpec((L, d), lambda b, c: (b * nc + c, 0)),
        out_shape=jax.ShapeDtypeStruct((t, d), F32),
        scratch_shapes=[pltpu.VMEM((SSM_GROUPS, SSM_STATE, GROUP_WIDTH), F32)],
        compiler_params=_params(("arbitrary", "arbitrary")),
        name="token_mixer",
    )(proj, dt, x2d, mod, alog, dskip_e, g_ssm_norm.reshape(1, D_INNER), expand, w_a, w_b, w_o)


def _ffn_input(x, mod_ref, g_ref):
    return _rms(x, g_ref[...]) * (1.0 + mod_ref[0, 4:5, :]) + mod_ref[0, 3:4, :]


def _router_kernel(x_ref, mod_ref, g_ref, wr_ref, br_ref, pos_ref, post_ref, w_ref, cnt_ref):
    tm = x_ref.shape[0]
    h = _ffn_input(x_ref[...], mod_ref, g_ref)
    h_hi = h.astype(BF16)
    h_lo = (h - h_hi.astype(F32)).astype(BF16)
    logits = (jnp.dot(h_hi, wr_ref[0], preferred_element_type=F32)
              + jnp.dot(h_lo, wr_ref[0], preferred_element_type=F32)
              + jnp.dot(h_hi, wr_ref[1], preferred_element_type=F32)) + br_ref[...]
    lane = lax.broadcasted_iota(jnp.int32, (tm, LANES), 1)
    work = jnp.where(lane < N_EXPERTS, logits, -jnp.inf)
    vals, hots = [], []
    for _ in range(TOP_K):
        v = jnp.max(work, axis=-1, keepdims=True)
        i = jnp.min(jnp.where(work == v, lane, LANES), axis=-1, keepdims=True)
        hot = lane == i
        vals.append(v)
        hots.append(hot)
        work = jnp.where(hot, -jnp.inf, work)
    exps = [jnp.exp(v - vals[0]) for v in vals]
    denom = exps[0] + exps[1] + exps[2] + exps[3]

    onehot = jnp.zeros((tm, LANES), F32)
    for hot in hots:
        onehot = onehot + jnp.where(hot, 1.0, 0.0)
    r = lax.broadcasted_iota(jnp.int32, (tm, tm), 0)
    c = lax.broadcasted_iota(jnp.int32, (tm, tm), 1)
    strict = jnp.where(c < r, 1.0, 0.0).astype(BF16)
    before = jnp.dot(strict, onehot.astype(BF16), preferred_element_type=F32)
    cnt = jnp.sum(onehot, axis=0, keepdims=True).astype(jnp.int32)
    seg_units = (cnt + (SEG_ALIGN - 1)) // SEG_ALIGN
    er = lax.broadcasted_iota(jnp.int32, (LANES, LANES), 0)
    ec = lax.broadcasted_iota(jnp.int32, (LANES, LANES), 1)
    upper = jnp.where(er < ec, 1.0, 0.0).astype(BF16)
    units8 = jnp.broadcast_to(seg_units.astype(F32), (8, LANES)).astype(BF16)
    seg_off = jnp.dot(units8, upper, preferred_element_type=F32)[0:1, :] * float(SEG_ALIGN)
    where_to = before + seg_off
    packed = jnp.zeros((tm, LANES), F32)
    for k in range(TOP_K):
        pos = jnp.sum(jnp.where(hots[k], where_to, 0.0), axis=-1, keepdims=True)
        pos_ref[:, k:k + 1] = pos.astype(jnp.int32)
        w_ref[:, k:k + 1] = exps[k] / denom
        packed = jnp.where(lane == k, pos, packed)
    post_ref[...] = packed.T[0:8, :].astype(jnp.int32)
    cnt_ref[0] = cnt


def _router(x1, mod, g_ffn, w_router, b_router, seq):
    t, d = x1.shape
    tm = MOE_TILE
    per_b = seq // tm
    wr = jnp.pad(w_router, ((0, 0), (0, LANES - N_EXPERTS)))
    wr_hi = wr.astype(BF16)
    wr = jnp.stack([wr_hi, (wr - wr_hi.astype(F32)).astype(BF16)])
    br = jnp.pad(b_router, (0, LANES - N_EXPERTS)).reshape(1, LANES)
    slot_spec = pl.BlockSpec((tm, TOP_K), lambda i: (i, 0))
    return pl.pallas_call(
        _router_kernel,
        grid=(t // tm,),
        in_specs=[pl.BlockSpec((tm, d), lambda i: (i, 0)),
                  pl.BlockSpec((1, N_MOD, d), lambda i: (i // per_b, 0, 0)),
                  pl.BlockSpec((1, d), lambda i: (0, 0)),
                  pl.BlockSpec((2, d, LANES), lambda i: (0, 0, 0)),
                  pl.BlockSpec((1, LANES), lambda i: (0, 0))],
        out_specs=[slot_spec, pl.BlockSpec((8, tm), lambda i: (0, i)), slot_spec,
                   pl.BlockSpec((1, 1, LANES), lambda i: (i, 0, 0))],
        out_shape=[jax.ShapeDtypeStruct((t, TOP_K), jnp.int32),
                   jax.ShapeDtypeStruct((8, t), jnp.int32),
                   jax.ShapeDtypeStruct((t, TOP_K), F32),
                   jax.ShapeDtypeStruct((t // tm, 1, LANES), jnp.int32)],
        compiler_params=_params(("arbitrary",)),
        name="router",
    )(x1, mod, g_ffn.reshape(1, d), wr, br)


def _segment_copies(tile, len_ref, loc_ref, glob_ref, make_copy, wait):
    if wait:
        end = tile * N_EXPERTS + N_EXPERTS - 1
        rows = loc_ref[end] + len_ref[end]
        make_copy(0, 0, pl.multiple_of(rows, SEG_ALIGN)).wait()
        return

    def per_expert(e, carry):
        j = tile * N_EXPERTS + e
        ln, lo, go = len_ref[j], loc_ref[j], glob_ref[j]

        @pl.when(ln > 0)
        def _():
            make_copy(pl.multiple_of(lo, SEG_ALIGN), pl.multiple_of(go, SEG_ALIGN),
                      pl.multiple_of(ln, SEG_ALIGN)).start()
        return carry

    lax.fori_loop(0, N_EXPERTS, per_expert, 0)


def _dispatch_kernel(len_ref, loc_ref, glob_ref, tail_ref, post_ref, x_ref, mod_ref, g_ref, xs_ref,
                     sort_scr, zero_scr, sem):
    i = pl.program_id(0)
    last = pl.num_programs(0) - 1
    slot = i % 2
    tm = x_ref.shape[0]
    r_max = sort_scr.shape[1]

    @pl.when(i == 0)
    def _():
        zero_scr[...] = jnp.zeros_like(zero_scr)
        for wait in (False, True):
            for e in range(N_EXPERTS):
                @pl.when(tail_ref[e] >= 0)
                def _():
                    cp = pltpu.make_async_copy(
                        zero_scr, xs_ref.at[pl.ds(pl.multiple_of(tail_ref[e], SEG_ALIGN), MOE_BLOCK), :],
                        sem.at[0])
                    cp.wait() if wait else cp.start()

    h = _ffn_input(x_ref[...], mod_ref, g_ref).astype(BF16)
    rows = lax.broadcasted_iota(jnp.int32, (r_max, tm), 0)
    perm = jnp.zeros((r_max, tm), F32)
    for k in range(TOP_K):
        perm = jnp.where(rows == post_ref[k:k + 1, :], 1.0, perm)
    sort_scr[slot] = jnp.dot(perm.astype(BF16), h, preferred_element_type=F32).astype(BF16)

    def copies_from(buf):
        def make_copy(lo, go, n):
            return pltpu.make_async_copy(sort_scr.at[buf, pl.ds(lo, n), :], xs_ref.at[pl.ds(go, n), :],
                                         sem.at[buf])
        return make_copy

    _segment_copies(i, len_ref, loc_ref, glob_ref, copies_from(slot), wait=False)

    @pl.when(i > 0)
    def _():
        _segment_copies(i - 1, len_ref, loc_ref, glob_ref, copies_from(1 - slot), wait=True)

    @pl.when(i == last)
    def _():
        _segment_copies(i, len_ref, loc_ref, glob_ref, copies_from(slot), wait=True)


def _dispatch(seg_len, seg_loc, seg_glob, tail, pos_t, x1, mod, g_ffn, r_tot, seq):
    t, d = x1.shape
    tm = MOE_TILE
    per_b = seq // tm
    return pl.pallas_call(
        _dispatch_kernel,
        grid_spec=pltpu.PrefetchScalarGridSpec(
            num_scalar_prefetch=4, grid=(t // tm,),
            in_specs=[pl.BlockSpec((8, tm), lambda i, *_: (0, i)),
                      pl.BlockSpec((tm, d), lambda i, *_: (i, 0)),
                      pl.BlockSpec((1, N_MOD, d), lambda i, *_: (i // per_b, 0, 0)),
                      pl.BlockSpec((1, d), lambda i, *_: (0, 0))],
            out_specs=pl.BlockSpec(memory_space=pl.ANY),
            scratch_shapes=[pltpu.VMEM((2, SORT_ROWS, d), BF16), pltpu.VMEM((MOE_BLOCK, d), BF16),
                            pltpu.SemaphoreType.DMA((2,))]),
        out_shape=jax.ShapeDtypeStruct((r_tot, d), BF16),
        compiler_params=_params(("arbitrary",)),
        name="dispatch",
    )(seg_len, seg_loc, seg_glob, tail, pos_t, x1, mod, g_ffn.reshape(1, d))


def _expert_kernel(be_ref, nu_ref, x_ref, wgu_ref, bgu_ref, wd_ref, bd_ref, o_ref, wgu_scr, wd_scr):
    i = pl.program_id(0)

    @pl.when(i < nu_ref[0])
    def _():
        @pl.when(jnp.logical_or(i == 0, be_ref[i] != be_ref[jnp.maximum(i - 1, 0)]))
        def _():
            wgu_scr[...] = wgu_ref[0].astype(BF16)
            wd_scr[...] = wd_ref[0].astype(BF16)

        gu = jnp.dot(x_ref[...], wgu_scr[...], preferred_element_type=F32) + bgu_ref[0]
        gate = jnp.minimum(gu[:, :D_FF], SWIGLU_LIMIT)
        up = jnp.clip(gu[:, D_FF:], -SWIGLU_LIMIT, SWIGLU_LIMIT)
        act = (up + 1.0) * gate * jax.nn.sigmoid(SWIGLU_ALPHA * gate)
        y = jnp.dot(act.astype(BF16), wd_scr[...], preferred_element_type=F32) + bd_ref[0]
        o_ref[...] = y.astype(BF16)


def _expert_blocks(block_e, n_used, xs, w_gu, b_gu, w_down, b_down):
    r_tot, d = xs.shape
    n_blocks = r_tot // MOE_BLOCK
    row_map = lambda i, be, nu: (jnp.minimum(i, nu[0] - 1), 0)
    e_map = lambda i, be, nu: (be[i], 0, 0)
    return pl.pallas_call(
        _expert_kernel,
        grid_spec=pltpu.PrefetchScalarGridSpec(
            num_scalar_prefetch=2, grid=(n_blocks,),
            in_specs=[pl.BlockSpec((MOE_BLOCK, d), row_map),
                      pl.BlockSpec((1, d, 2 * D_FF), e_map),
                      pl.BlockSpec((1, 1, 2 * D_FF), e_map),
                      pl.BlockSpec((1, D_FF, d), e_map),
                      pl.BlockSpec((1, 1, d), e_map)],
            out_specs=pl.BlockSpec((MOE_BLOCK, d), row_map),
            scratch_shapes=[pltpu.VMEM((d, 2 * D_FF), BF16), pltpu.VMEM((D_FF, d), BF16)]),
        out_shape=jax.ShapeDtypeStruct((r_tot, d), BF16),
        compiler_params=_params(("arbitrary",)),
        name="expert_blocks",
    )(block_e, n_used, xs, w_gu, b_gu.reshape(N_EXPERTS, 1, 2 * D_FF), w_down,
      b_down.reshape(N_EXPERTS, 1, d))


def _combine_kernel(len_ref, loc_ref, glob_ref, pos_ref, w_ref, x_ref, mod_ref, g_ref, y_ref, o_ref,
                    sort_scr, sem, *, final_norm):
    i = pl.program_id(0)
    last = pl.num_programs(0) - 1
    slot = i % 2
    tm = x_ref.shape[0]
    r_max = sort_scr.shape[1]

    def copies_into(buf):
        def make_copy(lo, go, n):
            return pltpu.make_async_copy(y_ref.at[pl.ds(go, n), :], sort_scr.at[buf, pl.ds(lo, n), :],
                                         sem.at[buf])
        return make_copy

    def fetch(tile, buf):
        sort_scr[buf] = jnp.zeros(sort_scr.shape[1:], sort_scr.dtype)
        _segment_copies(tile, len_ref, loc_ref, glob_ref, copies_into(buf), wait=False)

    @pl.when(i == 0)
    def _():
        fetch(i, slot)

    @pl.when(i < last)
    def _():
        fetch(i + 1, 1 - slot)

    _segment_copies(i, len_ref, loc_ref, glob_ref, copies_into(slot), wait=True)

    cols = lax.broadcasted_iota(jnp.int32, (tm, r_max), 1)
    w = w_ref[...]
    pos = pos_ref[...]
    gather_w = jnp.zeros((tm, r_max), F32)
    for k in range(TOP_K):
        gather_w = jnp.where(cols == pos[:, k:k + 1], w[:, k:k + 1], gather_w)
    moe = jnp.dot(gather_w.astype(BF16), sort_scr[slot], preferred_element_type=F32)
    x2 = x_ref[...] + mod_ref[0, 5:6, :] * moe
    o_ref[...] = _rms(x2, g_ref[...]) if final_norm else x2


def _combine(seg_len, seg_loc, seg_glob, pos, top_w, x1, mod, g_final, y, seq, final_norm):
    t, d = x1.shape
    tm = MOE_TILE
    per_b = seq // tm
    slot_spec = pl.BlockSpec((tm, TOP_K), lambda i, *_: (i, 0))
    return pl.pallas_call(
        functools.partial(_combine_kernel, final_norm=final_norm),
        grid_spec=pltpu.PrefetchScalarGridSpec(
            num_scalar_prefetch=3, grid=(t // tm,),
            in_specs=[slot_spec, slot_spec,
                      pl.BlockSpec((tm, d), lambda i, *_: (i, 0)),
                      pl.BlockSpec((1, N_MOD, d), lambda i, *_: (i // per_b, 0, 0)),
                      pl.BlockSpec((1, d), lambda i, *_: (0, 0)),
                      pl.BlockSpec(memory_space=pl.ANY)],
            out_specs=pl.BlockSpec((tm, d), lambda i, *_: (i, 0)),
            scratch_shapes=[pltpu.VMEM((2, SORT_ROWS, d), BF16), pltpu.SemaphoreType.DMA((2,))]),
        out_shape=jax.ShapeDtypeStruct((t, d), F32),
        compiler_params=_params(("arbitrary",)),
        name="combine",
    )(seg_len, seg_loc, seg_glob, pos, top_w, x1, mod, g_final.reshape(1, d), y)


def _regroup_w_in(w_in):
    o_sb, o_sc, o_sx = 0, D_MODEL, 2 * D_MODEL
    o_z = 3 * D_MODEL
    o_xbc = o_z + D_INNER
    o_dt = o_xbc + D_XBC
    o_ga = o_dt + SSM_HEADS
    o_gb = o_ga + D_MODEL
    main = jnp.concatenate([w_in[:, o_z:o_xbc], w_in[:, o_xbc:o_dt], w_in[:, o_sb:o_sc], w_in[:, o_sc:o_sx],
                            w_in[:, o_sx:o_z], w_in[:, o_ga:o_gb], w_in[:, o_gb:o_gb + D_MODEL]], axis=1)
    w_dt = jnp.pad(w_in[:, o_dt:o_ga], ((0, 0), (0, LANES - SSM_HEADS)))
    return main.astype(BF16), w_dt.astype(BF16)


def _segment_tables(counts, n_blocks):
    seg_len = (counts + SEG_ALIGN - 1) // SEG_ALIGN * SEG_ALIGN
    seg_loc = jnp.cumsum(seg_len, axis=1) - seg_len
    used = jnp.sum(seg_len, axis=0)
    padded = (used + MOE_BLOCK - 1) // MOE_BLOCK * MOE_BLOCK
    pend = jnp.cumsum(padded)
    pstart = pend - padded
    seg_glob = pstart[None, :] + jnp.cumsum(seg_len, axis=0) - seg_len
    tail = jnp.where(padded > 0, pend - MOE_BLOCK, -1)
    first_row = jnp.arange(n_blocks, dtype=jnp.int32) * MOE_BLOCK
    block_e = jnp.sum((pend[None, :] <= first_row[:, None]).astype(jnp.int32), axis=1)
    block_e = jnp.minimum(block_e, N_EXPERTS - 1)
    n_used = pend[-1:] // MOE_BLOCK
    i32 = lambda a: a.astype(jnp.int32)
    return (i32(seg_len).reshape(-1), i32(seg_loc).reshape(-1), i32(seg_glob).reshape(-1), i32(tail),
            i32(block_e), i32(n_used))


def kernel(x, c, w_ada, b_ada, g_mix, w_in, w_sconv, w_sconv_out, w_ssm_conv, b_ssm_conv, dt_bias, a_log,
           d_skip, g_ssm_norm, w_ssm_out, w_o, g_ffn, w_router, b_router, w_gu, b_gu, w_down, b_down,
           g_final):
    bsz, seq, d = x.shape
    depth = w_ada.shape[0]
    t = bsz * seq
    assert seq % MOE_TILE == 0 and seq % SSM_CHUNK == 0
    n_tiles = t // MOE_TILE
    worst_rows = t * TOP_K + n_tiles * N_EXPERTS * (SEG_ALIGN - 1) + N_EXPERTS * (MOE_BLOCK - 1)
    n_blocks = -(-worst_rows // MOE_BLOCK)
    r_tot = n_blocks * MOE_BLOCK
    xf = x.reshape(t, d)
    for l in range(depth):
        mod = _modulation(c, w_ada[l], b_ada[l]).reshape(bsz, N_MOD, d)
        w_main, w_dt = _regroup_w_in(w_in[l])
        proj, dt = _in_projection(xf, mod, g_mix[l], w_main, w_dt, w_sconv[l], w_ssm_conv[l], b_ssm_conv[l],
                                  dt_bias[l], seq)
        x1 = _token_mixer(proj, dt, xf, mod, a_log[l], d_skip[l], g_ssm_norm[l], w_sconv_out[l].astype(BF16),
                          w_ssm_out[l].astype(BF16), w_o[l].astype(BF16), bsz, seq)
        pos, pos_t, top_w, counts = _router(x1, mod, g_ffn[l], w_router[l], b_router[l], seq)
        seg_len, seg_loc, seg_glob, tail, block_e, n_used = _segment_tables(
            counts[:, 0, :N_EXPERTS], n_blocks)
        xs = _dispatch(seg_len, seg_loc, seg_glob, tail, pos_t, x1, mod, g_ffn[l], r_tot, seq)
        ys = _expert_blocks(block_e, n_used, xs, w_gu[l], b_gu[l], w_down[l], b_down[l])
        xf = _combine(seg_len, seg_loc, seg_glob, pos, top_w, x1, mod, g_final, ys, seq,
                      final_norm=(l == depth - 1))
    return xf.reshape(bsz, seq, d)
```
